```python
import math
import jax, jax.numpy as jnp
from jax import lax
import numpy as np

D_MODEL = 1024
BATCH = 8
SEQ = 4096
DEPTH = 1

LRU_HEADS = 8
LRU_BLOCK = D_MODEL // 8
D_LRU = LRU_HEADS * LRU_BLOCK
CONV_GROUPS = 4
D_CONV = CONV_GROUPS * (D_MODEL // 8)
D_MIX = D_LRU + D_CONV
D_IN = 2 * D_LRU + 3 * D_CONV
IN_SPLITS = (D_LRU, 2 * D_LRU, 2 * D_LRU + D_CONV, 2 * D_LRU + 2 * D_CONV)
LRU_CONV_W = 4
LRU_PADS = ((LRU_CONV_W - 1, 0), (0, LRU_CONV_W - 1))
LRU_C = 8.0
SHORT_CONV_W = 3

PEER_HEADS = 8
PEER_QDIM = 256
PEER_HALF = PEER_QDIM // 2
N_KEYS = 128
N_EXPERTS = N_KEYS * N_KEYS
PEER_TOPK = 16
PEER_CHUNK = 128

LN_EPS = 1e-5
RMS_EPS = 1e-6
ALPHA = (2.0 * DEPTH) ** 0.25
BETA = (8.0 * DEPTH) ** -0.25

kernel_name = "hymba_griffin_shortconv_peer_encoder_block"


def layer_norm(x, g, b):
    xf = x.astype(jnp.float32)
    mu = jnp.mean(xf, axis=-1, keepdims=True)
    var = jnp.mean(jnp.square(xf - mu), axis=-1, keepdims=True)
    y = (xf - mu) * lax.rsqrt(var + LN_EPS) * g.astype(jnp.float32) + b.astype(jnp.float32)
    return y.astype(x.dtype)


def rms_norm(x, g):
    xf = x.astype(jnp.float32)
    y = xf * lax.rsqrt(jnp.mean(jnp.square(xf), axis=-1, keepdims=True) + RMS_EPS) * g.astype(jnp.float32)
    return y.astype(x.dtype)


def depthwise_conv(x, w, pad):
    c = x.shape[-1]
    return lax.conv_general_dilated(
        x, w[:, None, :].astype(x.dtype), window_strides=(1,), padding=[pad],
        dimension_numbers=("NWC", "WIO", "NWC"), feature_group_count=c)


def block_diag_linear(x, w, b):
    bsz, s, d = x.shape
    xh = x.reshape(bsz, s, LRU_HEADS, LRU_BLOCK)
    y = jnp.einsum("bshi,hij->bshj", xh, w.astype(x.dtype)).reshape(bsz, s, d)
    return y + b.astype(x.dtype)


def _linear_recurrence_combine(left, right):
    a_l, b_l = left
    a_r, b_r = right
    return a_l * a_r, a_r * b_l + b_r


def rg_lru(xc, w_a, b_a, w_x, b_x, lam, reverse):
    r = jax.nn.sigmoid(block_diag_linear(xc, w_a, b_a).astype(jnp.float32))
    i = jax.nn.sigmoid(block_diag_linear(xc, w_x, b_x).astype(jnp.float32))
    log_a = -LRU_C * r * jax.nn.softplus(-lam.astype(jnp.float32))
    a = jnp.exp(log_a)
    u = jnp.sqrt(-jnp.expm1(2.0 * log_a)) * (i * xc.astype(jnp.float32))
    if reverse:
        a = jnp.flip(a, axis=1)
        u = jnp.flip(u, axis=1)
    _, h = lax.associative_scan(_linear_recurrence_combine, (a, u), axis=1)
    if reverse:
        h = jnp.flip(h, axis=1)
    return h.astype(xc.dtype)


def peer(x, w_q, k1, k2, u_tab, v_tab):
    bsz, s, d = x.shape
    t = bsz * s
    xt = x.reshape(t, d)
    q = (xt @ w_q.astype(x.dtype)).reshape(t, PEER_HEADS, 2, PEER_HALF)
    s1 = jnp.einsum("thd,kd->thk", q[:, :, 0], k1.astype(x.dtype)).astype(jnp.float32)
    s2 = jnp.einsum("thd,kd->thk", q[:, :, 1], k2.astype(x.dtype)).astype(jnp.float32)
    v1, i1 = lax.top_k(s1, PEER_TOPK)
    v2, i2 = lax.top_k(s2, PEER_TOPK)
    cand_s = (v1[..., :, None] + v2[..., None, :]).reshape(t, PEER_HEADS, PEER_TOPK * PEER_TOPK)
    cand_i = (i1[..., :, None] * N_KEYS + i2[..., None, :]).reshape(t, PEER_HEADS, PEER_TOPK * PEER_TOPK)
    top_s, pos = lax.top_k(cand_s, PEER_TOPK)
    idx = jnp.take_along_axis(cand_i, pos, axis=-1)
    g = jax.nn.softmax(top_s, axis=-1)

    n_blocks = t // PEER_CHUNK
    e_per_tok = PEER_HEADS * PEER_TOPK
    xb_all = xt.reshape(n_blocks, PEER_CHUNK, d)
    ib_all = idx.reshape(n_blocks, PEER_CHUNK, e_per_tok)
    gb_all = g.reshape(n_blocks, PEER_CHUNK, e_per_tok).astype(x.dtype)

    def expert_block(args):
        xb, ib, gb = args
        u = u_tab[ib].astype(x.dtype)
        act = jax.nn.gelu(jnp.einsum("cd,ced->ce", xb, u), approximate=False)
        v = v_tab[ib].astype(x.dtype)
        return jnp.einsum("ce,ced->cd", gb * act, v)

    y = lax.map(expert_block, (xb_all, ib_all, gb_all))
    return y.reshape(bsz, s, d)


def setup_inputs(seed: int = 0) -> dict:
    key = jax.random.key(seed)
    ks = jax.random.split(key, 28)
    f32 = jnp.float32
    L = DEPTH

    def nrm(k, shape, scale):
        return jax.random.normal(k, shape, f32) * scale

    a0 = jax.random.uniform(ks[10], (L, 2, D_LRU), f32, minval=0.9, maxval=0.999)
    p = a0 ** (1.0 / LRU_C)
    lam = jnp.log(p) - jnp.log1p(-p)

    return {
        "x": nrm(ks[0], (BATCH, SEQ, D_MODEL), 1.0),
        "ln_emb_g": 1.0 + nrm(ks[1], (D_MODEL,), 0.02),
        "ln_emb_b": nrm(ks[2], (D_MODEL,), 0.02),
        "w_in": nrm(ks[3], (L, D_MODEL, D_IN), D_MODEL ** -0.5),
        "lru_conv_w": nrm(ks[4], (L, 2, LRU_CONV_W, D_LRU), LRU_CONV_W ** -0.5),
        "lru_conv_b": nrm(ks[5], (L, 2, D_LRU), 0.01),
        "lru_wa": nrm(ks[6], (L, 2, LRU_HEADS, LRU_BLOCK, LRU_BLOCK), LRU_BLOCK ** -0.5),
        "lru_ba": nrm(ks[7], (L, 2, D_LRU), 0.01),
        "lru_wx": nrm(ks[8], (L, 2, LRU_HEADS, LRU_BLOCK, LRU_BLOCK), LRU_BLOCK ** -0.5),
        "lru_bx": nrm(ks[9], (L, 2, D_LRU), 0.01),
        "lru_lambda": lam,
        "conv_w": nrm(ks[11], (L, SHORT_CONV_W, D_CONV), SHORT_CONV_W ** -0.5),
        "norm_lru_g": 1.0 + nrm(ks[12], (L, D_LRU), 0.02),
        "norm_conv_g": 1.0 + nrm(ks[13], (L, D_CONV), 0.02),
        "w_out": nrm(ks[14], (L, D_MIX, D_MODEL), BETA * D_MIX ** -0.5),
        "ln1_g": 1.0 + nrm(ks[15], (L, D_MODEL), 0.02),
        "ln1_b": nrm(ks[16], (L, D_MODEL), 0.02),
        "peer_wq": nrm(ks[17], (L, D_MODEL, PEER_HEADS * PEER_QDIM), D_MODEL ** -0.5),
        "peer_k1": nrm(ks[18], (L, N_KEYS, PEER_HALF), PEER_HALF ** -0.5),
        "peer_k2": nrm(ks[19], (L, N_KEYS, PEER_HALF), PEER_HALF ** -0.5),
        "peer_u": nrm(ks[20], (L, N_EXPERTS, D_MODEL), D_MODEL ** -0.5),
        "peer_v": nrm(ks[21], (L, N_EXPERTS, D_MODEL), BETA),
        "ln2_g": 1.0 + nrm(ks[22], (L, D_MODEL), 0.02),
        "ln2_b": nrm(ks[23], (L, D_MODEL), 0.02),
    }


def reference(x, ln_emb_g, ln_emb_b, w_in, lru_conv_w, lru_conv_b, lru_wa, lru_ba, lru_wx, lru_bx,
              lru_lambda, conv_w, norm_lru_g, norm_conv_g, w_out, ln1_g, ln1_b,
              peer_wq, peer_k1, peer_k2, peer_u, peer_v, ln2_g, ln2_b):
    h = layer_norm(x, ln_emb_g, ln_emb_b)
    for l in range(DEPTH):
        z = h @ w_in[l].astype(h.dtype)
        xr, gate, cb, cc, cx = jnp.split(z, IN_SPLITS, axis=-1)
        lru = None
        for d, pad in enumerate(LRU_PADS):
            xc = depthwise_conv(xr, lru_conv_w[l, d], pad) + lru_conv_b[l, d].astype(xr.dtype)
            hd = rg_lru(xc, lru_wa[l, d], lru_ba[l, d], lru_wx[l, d], lru_bx[l, d],
                        lru_lambda[l, d], reverse=(d == 1))
            lru = hd if lru is None else lru + hd
        y_lru = jax.nn.gelu(gate) * lru
        y_conv = cb * depthwise_conv(cc * cx, conv_w[l], (1, 1))
        mix = jnp.concatenate([rms_norm(y_lru, norm_lru_g[l]),
                               rms_norm(y_conv, norm_conv_g[l])], axis=-1)
        h = layer_norm(ALPHA * h + mix @ w_out[l].astype(h.dtype), ln1_g[l], ln1_b[l])
        f = peer(h, peer_wq[l], peer_k1[l], peer_k2[l], peer_u[l], peer_v[l])
        h = layer_norm(ALPHA * h + f, ln2_g[l], ln2_b[l])
    return h
```

```python
import functools
import math

import jax
import jax.numpy as jnp
from jax import lax
from jax.experimental import pallas as pl
from jax.experimental.pallas import tpu as pltpu

F32 = jnp.float32
BF16 = jnp.bfloat16

SUBLANES = 8
LANES = 128
VMEM_LIMIT_BYTES = 56 * 1024 * 1024

LRU_HEADS = 8
LRU_CONV_W = 4
LRU_C = 8.0
SHORT_CONV_W = 3
PEER_HEADS = 8
PEER_TOPK = 16
LN_EPS = 1e-5
RMS_EPS = 1e-6
SQRT_HALF = 0.7071067811865476

STEPS_PER_CHUNK = 32
ROUTE_TOKENS = 512
TOKEN_GROUP = 128
HALF_ROWS = 4
SLOT_STRIDE = TOKEN_GROUP + SUBLANES


def _layer_norm(x, g, b):
    mu = jnp.mean(x, axis=-1, keepdims=True)
    xc = x - mu
    var = jnp.mean(xc * xc, axis=-1, keepdims=True)
    return xc * lax.rsqrt(var + LN_EPS) * g + b


def _rms_norm(x, g):
    return x * lax.rsqrt(jnp.mean(x * x, axis=-1, keepdims=True) + RMS_EPS) * g


def _gelu(x):
    return 0.5 * x * (1.0 + lax.erf(x * SQRT_HALF))


def _softplus(y):
    return jnp.maximum(y, 0.0) + jnp.log1p(jnp.exp(-jnp.abs(y)))


def _lru_coeffs(xc, wa_ref, ba_ref, wx_ref, bx_ref, lam_ref):
    blk = wa_ref.shape[-1]
    xcb = xc.astype(BF16)
    rs, gs = [], []
    for hd in range(LRU_HEADS):
        xh = xcb[:, hd * blk:(hd + 1) * blk]
        rs.append(jnp.dot(xh, wa_ref[hd], preferred_element_type=F32))
        gs.append(jnp.dot(xh, wx_ref[hd], preferred_element_type=F32))
    r = jax.nn.sigmoid(jnp.concatenate(rs, axis=1) + ba_ref[...])
    gate_i = jax.nn.sigmoid(jnp.concatenate(gs, axis=1) + bx_ref[...])
    log_a = -LRU_C * r * _softplus(-lam_ref[...])
    a = jnp.exp(log_a)
    y = 2.0 * log_a
    e = a * a
    neg_expm1 = jnp.where(e == 1.0, -y, (1.0 - e) * y / jnp.log(e))
    u = jnp.sqrt(neg_expm1) * (gate_i * xc)
    return a, u


def _scan_chunk(a_s, u_s, h_s, steps, reverse):
    unroll = 8

    def body(k, h):
        for q in range(unroll):
            step = k * unroll + q
            if reverse:
                step = steps - 1 - step
            row = pl.multiple_of(step * SUBLANES, SUBLANES)
            h = a_s[pl.ds(row, SUBLANES), :] * h + u_s[pl.ds(row, SUBLANES), :]
            u_s[pl.ds(row, SUBLANES), :] = h
        return h

    h_s[...] = lax.fori_loop(0, steps // unroll, body, h_s[...])


def _lru_bwd_kernel(x_ref, lng_ref, lnb_ref, wxr_ref, cw_ref, cb_ref, wa_ref, ba_ref, wx_ref, bx_ref,
                    lam_ref, hb_ref, a_s, u_s, h_s, halo_s):
    rows = x_ref.shape[0]
    halo_rows = halo_s.shape[0]

    @pl.when(pl.program_id(0) == 0)
    def _():
        h_s[...] = jnp.zeros_like(h_s)
        halo_s[...] = jnp.zeros_like(halo_s)

    h0 = _layer_norm(x_ref[...], lng_ref[...], lnb_ref[...])
    xr = jnp.dot(h0.astype(BF16), wxr_ref[...], preferred_element_type=F32)
    ext = jnp.concatenate([xr, halo_s[...]], axis=0)
    xc = cb_ref[...]
    for k in range(LRU_CONV_W):
        xc = xc + cw_ref[k:k + 1, :] * ext[k * SUBLANES:k * SUBLANES + rows, :]
    halo_s[...] = xr[0:halo_rows, :]
    a, u = _lru_coeffs(xc, wa_ref, ba_ref, wx_ref, bx_ref, lam_ref)
    a_s[...] = a
    u_s[...] = u
    _scan_chunk(a_s, u_s, h_s, rows // SUBLANES, reverse=True)
    hb_ref[...] = u_s[...]


def _mixer_kernel(x_ref, xn_ref, hb_ref, lng_ref, lnb_ref, win_ref, cw_ref, cb_ref, wa_ref, ba_ref,
                  wx_ref, bx_ref, lam_ref, scw_ref, gl_ref, gc_ref, wout_ref, l1g_ref, l1b_ref,
                  o_ref, a_s, u_s, h_s, halo_s, phalo_s, *, alpha, d_lru, d_conv):
    rows = x_ref.shape[0]
    halo_rows = halo_s.shape[0]
    i = pl.program_id(0)

    @pl.when(i == 0)
    def _():
        h_s[...] = jnp.zeros_like(h_s)
        halo_s[...] = jnp.zeros_like(halo_s)
        phalo_s[...] = jnp.zeros_like(phalo_s)

    h0 = _layer_norm(x_ref[...], lng_ref[...], lnb_ref[...])
    z = jnp.dot(h0.astype(BF16), win_ref[...], preferred_element_type=F32)
    xr = z[:, 0:d_lru]
    gate = z[:, d_lru:2 * d_lru]
    c0 = 2 * d_lru
    cb = z[:, c0:c0 + d_conv]
    cc = z[:, c0 + d_conv:c0 + 2 * d_conv]
    cx = z[:, c0 + 2 * d_conv:c0 + 3 * d_conv]

    ext = jnp.concatenate([halo_s[...], xr], axis=0)
    xc = cb_ref[...]
    for k in range(LRU_CONV_W):
        xc = xc + cw_ref[k:k + 1, :] * ext[k * SUBLANES:k * SUBLANES + rows, :]
    halo_s[...] = xr[rows - halo_rows:rows, :]
    a, u = _lru_coeffs(xc, wa_ref, ba_ref, wx_ref, bx_ref, lam_ref)
    a_s[...] = a
    u_s[...] = u
    _scan_chunk(a_s, u_s, h_s, rows // SUBLANES, reverse=False)
    y_lru = _gelu(gate) * (u_s[...] + hb_ref[...])

    p = cc * cx
    h0n = _layer_norm(xn_ref[...], lng_ref[...], lnb_ref[...])
    zn = jnp.dot(h0n.astype(BF16), win_ref[:, c0 + d_conv:c0 + 3 * d_conv], preferred_element_type=F32)
    pn = zn[:, 0:d_conv] * zn[:, d_conv:2 * d_conv]
    pn = jnp.where(i == pl.num_programs(0) - 1, 0.0, pn)
    pext = jnp.concatenate([phalo_s[...], p, pn], axis=0)
    yc = scw_ref[0:1, :] * pext[0:rows, :]
    for k in range(1, SHORT_CONV_W):
        yc = yc + scw_ref[k:k + 1, :] * pext[k * SUBLANES:k * SUBLANES + rows, :]
    phalo_s[...] = p[rows - SUBLANES:rows, :]
    y_conv = cb * yc

    ml = _rms_norm(y_lru, gl_ref[...]).astype(BF16)
    mc = _rms_norm(y_conv, gc_ref[...]).astype(BF16)
    o = (jnp.dot(ml, wout_ref[0:d_lru, :], preferred_element_type=F32)
         + jnp.dot(mc, wout_ref[d_lru:d_lru + d_conv, :], preferred_element_type=F32))
    o_ref[...] = _layer_norm(alpha * h0 + o, l1g_ref[...], l1b_ref[...])


def _top_rows(s, count):
    n = s.shape[0]
    iota = lax.broadcasted_iota(jnp.int32, s.shape, 0)
    vals, idxs = [], []
    for _ in range(count):
        m = jnp.max(s, axis=0, keepdims=True)
        sel = jnp.min(jnp.where(s == m, iota, n), axis=0, keepdims=True)
        vals.append(m)
        idxs.append(sel)
        s = jnp.where(iota == sel, -jnp.inf, s)
    return jnp.concatenate(vals, axis=0), jnp.concatenate(idxs, axis=0)


def _route_kernel(h_ref, wq_ref, k1_ref, k2_ref, idx_ref, g_ref, s1_s, s2_s, *, n_keys):
    groups = idx_ref.shape[0]
    half = k1_ref.shape[1]
    q = jnp.dot(h_ref[...].astype(BF16), wq_ref[...], preferred_element_type=F32).astype(BF16)
    nt = (((1,), (1,)), ((), ()))
    s1 = lax.dot_general(k1_ref[...], q[:, 0:half], nt, preferred_element_type=F32)
    s2 = lax.dot_general(k2_ref[...], q[:, half:2 * half], nt, preferred_element_type=F32)
    for c in range(groups):
        s1_s[c] = s1[:, c * LANES:(c + 1) * LANES]
        s2_s[c] = s2[:, c * LANES:(c + 1) * LANES]

    def group(c, carry):
        v1, i1 = _top_rows(s1_s[c], PEER_TOPK)
        v2, i2 = _top_rows(s2_s[c], PEER_TOPK)
        cand = jnp.concatenate([v1[a:a + 1, :] + v2 for a in range(PEER_TOPK)], axis=0)
        cand_e = jnp.concatenate([i1[a:a + 1, :] * n_keys + i2 for a in range(PEER_TOPK)], axis=0)
        n = cand.shape[0]
        iota = lax.broadcasted_iota(jnp.int32, cand.shape, 0)
        tops, ids = [], []
        for _ in range(PEER_TOPK):
            m = jnp.max(cand, axis=0, keepdims=True)
            pos = jnp.min(jnp.where(cand == m, iota, n), axis=0, keepdims=True)
            hit = iota == pos
            tops.append(m)
            ids.append(jnp.max(jnp.where(hit, cand_e, -1), axis=0, keepdims=True))
            cand = jnp.where(hit, -jnp.inf, cand)
        top = jnp.concatenate(tops, axis=0)
        ex = jnp.exp(top - top[0:1, :])
        g_ref[c] = ex / jnp.sum(ex, axis=0, keepdims=True)
        idx_ref[c] = jnp.concatenate(ids, axis=0)
        return carry

    lax.fori_loop(0, groups, group, 0)


def _unpack(words):
    lo = lax.bitcast_convert_type(words << 16, F32)
    hi = lax.bitcast_convert_type(words & jnp.uint32(0xFFFF0000), F32)
    return lo, hi


def _expert_u_kernel(idx_ref, x_ref, g_ref, tab_ref, w_ref, slot_s):
    n_pairs = idx_ref.shape[1]
    lane = lax.broadcasted_iota(jnp.int32, (n_pairs, TOKEN_GROUP), 1)

    def token(t, act):
        xt = x_ref[t]
        x_lo = xt[0:HALF_ROWS, :]
        x_hi = xt[HALF_ROWS:2 * HALF_ROWS, :]
        for j in range(n_pairs):
            row = pl.multiple_of(idx_ref[0, j, t] * HALF_ROWS, HALF_ROWS)
            lo, hi = _unpack(tab_ref[pl.ds(row, HALF_ROWS), :])
            slot_s[pl.ds(j, HALF_ROWS, stride=SLOT_STRIDE), :] = lo * x_lo + hi * x_hi
        part = slot_s[0:n_pairs, :]
        for s in range(1, HALF_ROWS):
            part = part + slot_s[s * SLOT_STRIDE:s * SLOT_STRIDE + n_pairs, :]
        dots = jnp.sum(part, axis=1, keepdims=True)
        return jnp.where(lane == t, dots, act)

    act = lax.fori_loop(0, TOKEN_GROUP, token, jnp.zeros((n_pairs, TOKEN_GROUP), F32))
    w_ref[0] = g_ref[0] * _gelu(act)


def _expert_v_kernel(idx_ref, w_ref, x_ref, tab_ref, lng_ref, lnb_ref, o_ref, *, alpha):
    n_pairs = idx_ref.shape[1]
    n_acc = 4

    def token(t, carry):
        acc_lo = [jnp.zeros((HALF_ROWS, LANES), F32) for _ in range(n_acc)]
        acc_hi = [jnp.zeros((HALF_ROWS, LANES), F32) for _ in range(n_acc)]
        for j in range(n_pairs):
            row = pl.multiple_of(idx_ref[0, j, t] * HALF_ROWS, HALF_ROWS)
            wj = w_ref[0, j, t]
            lo, hi = _unpack(tab_ref[pl.ds(row, HALF_ROWS), :])
            acc_lo[j % n_acc] = acc_lo[j % n_acc] + wj * lo
            acc_hi[j % n_acc] = acc_hi[j % n_acc] + wj * hi
        y_lo = (acc_lo[0] + acc_lo[1]) + (acc_lo[2] + acc_lo[3])
        y_hi = (acc_hi[0] + acc_hi[1]) + (acc_hi[2] + acc_hi[3])
        o_ref[t] = alpha * x_ref[t] + jnp.concatenate([y_lo, y_hi], axis=0)
        return carry

    lax.fori_loop(0, TOKEN_GROUP, token, 0)
    h = o_ref[...]
    d = h.shape[1] * h.shape[2]
    mu = jnp.sum(jnp.sum(h, axis=2, keepdims=True), axis=1, keepdims=True) / d
    hc = h - mu
    var = jnp.sum(jnp.sum(hc * hc, axis=2, keepdims=True), axis=1, keepdims=True) / d
    o_ref[...] = hc * lax.rsqrt(var + LN_EPS) * lng_ref[...] + lnb_ref[...]


def _pack_table(tab):
    e, d = tab.shape
    bits = lax.bitcast_convert_type(tab.astype(BF16), jnp.uint16).astype(jnp.uint32)
    bits = bits.reshape(e, 2, d // 2)
    words = bits[:, 0, :] | (bits[:, 1, :] << 16)
    return words.reshape(e * (d // 2 // LANES), LANES)


def _const_spec(shape):
    zeros = (0,) * len(shape)
    return pl.BlockSpec(shape, lambda *_: zeros, pipeline_mode=pl.Buffered(1))


def _row(v):
    return v.reshape(1, -1).astype(F32)


def _params(semantics):
    return pltpu.CompilerParams(dimension_semantics=semantics, vmem_limit_bytes=VMEM_LIMIT_BYTES)


def kernel(x, ln_emb_g, ln_emb_b, w_in, lru_conv_w, lru_conv_b, lru_wa, lru_ba, lru_wx, lru_bx, lru_lambda, conv_w, norm_lru_g, norm_conv_g, w_out, ln1_g, ln1_b, peer_wq, peer_k1, peer_k2, peer_u, peer_v, ln2_g, ln2_b):
    batch, seq, d_model = x.shape
    depth = w_in.shape[0]
    assert batch == SUBLANES and d_model == SUBLANES * LANES
    assert depth == 1
    d_lru = lru_conv_w.shape[-1]
    d_conv = conv_w.shape[-1]
    n_keys, half = peer_k1.shape[1], peer_k1.shape[2]
    n_tokens = batch * seq
    steps = min(STEPS_PER_CHUNK, seq)
    rows = steps * batch
    n_chunks = seq // steps
    assert seq % steps == 0 and steps % 8 == 0
    route_tokens = min(ROUTE_TOKENS, n_tokens)
    assert n_tokens % route_tokens == 0 and route_tokens % TOKEN_GROUP == 0
    n_groups = n_tokens // TOKEN_GROUP
    halo_rows = (LRU_CONV_W - 1) * batch
    alpha = (2.0 * depth) ** 0.25

    h = x.transpose(1, 0, 2).reshape(n_tokens, d_model)
    ln_g, ln_b = _row(ln_emb_g), _row(ln_emb_b)

    for l in range(depth):
        w_in_b = w_in[l].astype(BF16)
        wa_b, wx_b = lru_wa[l].astype(BF16), lru_wx[l].astype(BF16)
        lru_small = lambda d: [lru_conv_w[l, d], _row(lru_conv_b[l, d]), wa_b[d], _row(lru_ba[l, d]),
                               wx_b[d], _row(lru_bx[l, d]), _row(lru_lambda[l, d])]
        lru_specs = [_const_spec((LRU_CONV_W, d_lru)), _const_spec((1, d_lru)),
                     _const_spec(wa_b.shape[1:]), _const_spec((1, d_lru)),
                     _const_spec(wx_b.shape[1:]), _const_spec((1, d_lru)), _const_spec((1, d_lru))]
        lru_scratch = [pltpu.VMEM((rows, d_lru), F32), pltpu.VMEM((rows, d_lru), F32),
                       pltpu.VMEM((batch, d_lru), F32), pltpu.VMEM((halo_rows, d_lru), F32)]

        hb = pl.pallas_call(
            _lru_bwd_kernel,
            grid=(n_chunks,),
            in_specs=[pl.BlockSpec((rows, d_model), lambda i: (n_chunks - 1 - i, 0)),
                      _const_spec((1, d_model)), _const_spec((1, d_model)),
                      _const_spec((d_model, d_lru))] + lru_specs,
            out_specs=pl.BlockSpec((rows, d_lru), lambda i: (n_chunks - 1 - i, 0)),
            out_shape=jax.ShapeDtypeStruct((n_tokens, d_lru), F32),
            scratch_shapes=lru_scratch,
            compiler_params=_params(("arbitrary",)),
            name="lru_bwd",
        )(h, ln_g, ln_b, w_in_b[:, 0:d_lru], *lru_small(1))

        h1 = pl.pallas_call(
            functools.partial(_mixer_kernel, alpha=alpha, d_lru=d_lru, d_conv=d_conv),
            grid=(n_chunks,),
            in_specs=[pl.BlockSpec((rows, d_model), lambda i: (i, 0)),
                      pl.BlockSpec((batch, d_model), lambda i: (jnp.minimum((i + 1) * steps, seq - 1), 0)),
                      pl.BlockSpec((rows, d_lru), lambda i: (i, 0)),
                      _const_spec((1, d_model)), _const_spec((1, d_model)),
                      _const_spec(w_in_b.shape)] + lru_specs +
                     [_const_spec((SHORT_CONV_W, d_conv)), _const_spec((1, d_lru)), _const_spec((1, d_conv)),
                      _const_spec(w_out.shape[1:]), _const_spec((1, d_model)), _const_spec((1, d_model))],
            out_specs=pl.BlockSpec((rows, d_model), lambda i: (i, 0)),
            out_shape=jax.ShapeDtypeStruct((n_tokens, d_model), F32),
            scratch_shapes=lru_scratch + [pltpu.VMEM((batch, d_conv), F32)],
            compiler_params=_params(("arbitrary",)),
            name="mixer",
        )(h, h, hb, ln_g, ln_b, w_in_b, *lru_small(0), conv_w[l], _row(norm_lru_g[l]), _row(norm_conv_g[l]),
          w_out[l].astype(BF16), _row(ln1_g[l]), _row(ln1_b[l]))

        groups_per_step = route_tokens // TOKEN_GROUP
        n_pairs = PEER_HEADS * PEER_TOPK
        idx3, g3 = pl.pallas_call(
            functools.partial(_route_kernel, n_keys=n_keys),
            grid=(n_tokens // route_tokens, PEER_HEADS),
            in_specs=[pl.BlockSpec((route_tokens, d_model), lambda i, hd: (i, 0)),
                      pl.BlockSpec((d_model, 2 * half), lambda i, hd: (0, hd)),
                      _const_spec((n_keys, half)), _const_spec((n_keys, half))],
            out_specs=[pl.BlockSpec((groups_per_step, PEER_TOPK, TOKEN_GROUP), lambda i, hd: (i, hd, 0)),
                       pl.BlockSpec((groups_per_step, PEER_TOPK, TOKEN_GROUP), lambda i, hd: (i, hd, 0))],
            out_shape=[jax.ShapeDtypeStruct((n_groups, n_pairs, TOKEN_GROUP), jnp.int32),
                       jax.ShapeDtypeStruct((n_groups, n_pairs, TOKEN_GROUP), F32)],
            scratch_shapes=[pltpu.VMEM((groups_per_step, n_keys, LANES), F32),
                            pltpu.VMEM((groups_per_step, n_keys, LANES), F32)],
            compiler_params=_params(("arbitrary", "arbitrary")),
            name="route",
        )(h1, peer_wq[l].astype(BF16), peer_k1[l].astype(BF16), peer_k2[l].astype(BF16))

        u_tab = _pack_table(peer_u[l])
        v_tab = _pack_table(peer_v[l])
        h1_3 = h1.reshape(n_tokens, SUBLANES, LANES)
        group_spec = lambda space: pl.BlockSpec((1, n_pairs, TOKEN_GROUP), lambda i: (i, 0, 0), memory_space=space)
        tok_spec = pl.BlockSpec((TOKEN_GROUP, SUBLANES, LANES), lambda i: (i, 0, 0))

        w3 = pl.pallas_call(
            _expert_u_kernel,
            grid=(n_groups,),
            in_specs=[group_spec(pltpu.SMEM), tok_spec, group_spec(pltpu.VMEM), _const_spec(u_tab.shape)],
            out_specs=group_spec(pltpu.VMEM),
            out_shape=jax.ShapeDtypeStruct((n_groups, n_pairs, TOKEN_GROUP), F32),
            scratch_shapes=[pltpu.VMEM((HALF_ROWS * SLOT_STRIDE, LANES), F32)],
            compiler_params=_params(("arbitrary",)),
            name="expert_u",
        )(idx3, h1_3, g3, u_tab)

        h2 = pl.pallas_call(
            functools.partial(_expert_v_kernel, alpha=alpha),
            grid=(n_groups,),
            in_specs=[group_spec(pltpu.SMEM), group_spec(pltpu.SMEM), tok_spec, _const_spec(v_tab.shape),
                      _const_spec((SUBLANES, LANES)), _const_spec((SUBLANES, LANES))],
            out_specs=tok_spec,
            out_shape=jax.ShapeDtypeStruct((n_tokens, SUBLANES, LANES), F32),
            compiler_params=_params(("arbitrary",)),
            name="expert_v",
        )(idx3, w3, h1_3, v_tab, ln2_g[l].reshape(SUBLANES, LANES), ln2_b[l].reshape(SUBLANES, LANES))

        h = h2.reshape(n_tokens, d_model)

    return h.reshape(seq, batch, d_model).transpose(1, 0, 2)
```

```python
import functools
import math

import jax
import jax.numpy as jnp
from jax import lax
from jax.experimental import pallas as pl
from jax.experimental.pallas import tpu as pltpu

F32 = jnp.float32
BF16 = jnp.bfloat16

SUBLANES = 8
LANES = 128
VMEM_LIMIT_BYTES = 56 * 1024 * 1024

LRU_HEADS = 8
LRU_CONV_W = 4
LRU_C = 8.0
SHORT_CONV_W = 3
PEER_HEADS = 8
PEER_TOPK = 16
LN_EPS = 1e-5
RMS_EPS = 1e-6
SQRT_HALF = 0.7071067811865476

STEPS_PER_CHUNK = 32
ROUTE_TOKENS = 512
TOKEN_GROUP = 128
HALF_ROWS = 4
SLOT_STRIDE = TOKEN_GROUP + SUBLANES


def _layer_norm(x, g, b):
    mu = jnp.mean(x, axis=-1, keepdims=True)
    xc = x - mu
    var = jnp.mean(xc * xc, axis=-1, keepdims=True)
    return xc * lax.rsqrt(var + LN_EPS) * g + b


def _rms_norm(x, g):
    return x * lax.rsqrt(jnp.mean(x * x, axis=-1, keepdims=True) + RMS_EPS) * g


def _gelu(x):
    return 0.5 * x * (1.0 + lax.erf(x * SQRT_HALF))


def _softplus(y):
    return jnp.maximum(y, 0.0) + jnp.log1p(jnp.exp(-jnp.abs(y)))


def _lru_coeffs(xc, wa_ref, ba_ref, wx_ref, bx_ref, lam_ref):
    blk = wa_ref.shape[-1]
    xcb = xc.astype(BF16)
    rs, gs = [], []
    for hd in range(LRU_HEADS):
        xh = xcb[:, hd * blk:(hd + 1) * blk]
        rs.append(jnp.dot(xh, wa_ref[hd], preferred_element_type=F32))
        gs.append(jnp.dot(xh, wx_ref[hd], preferred_element_type=F32))
    r = jax.nn.sigmoid(jnp.concatenate(rs, axis=1) + ba_ref[...])
    gate_i = jax.nn.sigmoid(jnp.concatenate(gs, axis=1) + bx_ref[...])
    log_a = -LRU_C * r * _softplus(-lam_ref[...])
    a = jnp.exp(log_a)
    y = 2.0 * log_a
    e = a * a
    neg_expm1 = jnp.where(e == 1.0, -y, (1.0 - e) * y / jnp.log(e))
    u = jnp.sqrt(neg_expm1) * (gate_i * xc)
    return a, u


def _scan_chunk(a_s, u_s, h_s, steps, reverse):
    unroll = 8

    def body(k, h):
        for q in range(unroll):
            step = k * unroll + q
            if reverse:
                step = steps - 1 - step
            row = pl.multiple_of(step * SUBLANES, SUBLANES)
            h = a_s[pl.ds(row, SUBLANES), :] * h + u_s[pl.ds(row, SUBLANES), :]
            u_s[pl.ds(row, SUBLANES), :] = h
        return h

    h_s[...] = lax.fori_loop(0, steps // unroll, body, h_s[...])


def _lru_bwd_kernel(x_ref, lng_ref, lnb_ref, wxr_ref, cw_ref, cb_ref, wa_ref, ba_ref, wx_ref, bx_ref,
                    lam_ref, hb_ref, a_s, u_s, h_s, halo_s):
    rows = x_ref.shape[0]
    halo_rows = halo_s.shape[0]

    @pl.when(pl.program_id(0) == 0)
    def _():
        h_s[...] = jnp.zeros_like(h_s)
        halo_s[...] = jnp.zeros_like(halo_s)

    h0 = _layer_norm(x_ref[...], lng_ref[...], lnb_ref[...])
    xr = jnp.dot(h0.astype(BF16), wxr_ref[...], preferred_element_type=F32)
    ext = jnp.concatenate([xr, halo_s[...]], axis=0)
    xc = cb_ref[...]
    for k in range(LRU_CONV_W):
        xc = xc + cw_ref[k:k + 1, :] * ext[k * SUBLANES:k * SUBLANES + rows, :]
    halo_s[...] = xr[0:halo_rows, :]
    a, u = _lru_coeffs(xc, wa_ref, ba_ref, wx_ref, bx_ref, lam_ref)
    a_s[...] = a
    u_s[...] = u
    _scan_chunk(a_s, u_s, h_s, rows // SUBLANES, reverse=True)
    hb_ref[...] = u_s[...]


def _mixer_kernel(x_ref, xn_ref, hb_ref, lng_ref, lnb_ref, win_ref, cw_ref, cb_ref, wa_ref, ba_ref,
                  wx_ref, bx_ref, lam_ref, scw_ref, gl_ref, gc_ref, wout_ref, l1g_ref, l1b_ref,
                  o_ref, a_s, u_s, h_s, halo_s, phalo_s, *, alpha, d_lru, d_conv):
    rows = x_ref.shape[0]
    halo_rows = halo_s.shape[0]
    i = pl.program_id(0)

    @pl.when(i == 0)
    def _():
        h_s[...] = jnp.zeros_like(h_s)
        halo_s[...] = jnp.zeros_like(halo_s)
        phalo_s[...] = jnp.zeros_like(phalo_s)

    h0 = _layer_norm(x_ref[...], lng_ref[...], lnb_ref[...])
    z = jnp.dot(h0.astype(BF16), win_ref[...], preferred_element_type=F32)
    xr = z[:, 0:d_lru]
    gate = z[:, d_lru:2 * d_lru]
    c0 = 2 * d_lru
    cb = z[:, c0:c0 + d_conv]
    cc = z[:, c0 + d_conv:c0 + 2 * d_conv]
    cx = z[:, c0 + 2 * d_conv:c0 + 3 * d_conv]

    ext = jnp.concatenate([halo_s[...], xr], axis=0)
    xc = cb_ref[...]
    for k in range(LRU_CONV_W):
        xc = xc + cw_ref[k:k + 1, :] * ext[k * SUBLANES:k * SUBLANES + rows, :]
    halo_s[...] = xr[rows - halo_rows:rows, :]
    a, u = _lru_coeffs(xc, wa_ref, ba_ref, wx_ref, bx_ref, lam_ref)
    a_s[...] = a
    u_s[...] = u
    _scan_chunk(a_s, u_s, h_s, rows // SUBLANES, reverse=False)
    y_lru = _gelu(gate) * (u_s[...] + hb_ref[...])

    p = cc * cx
    h0n = _layer_norm(xn_ref[...], lng_ref[...], lnb_ref[...])
    zn = jnp.dot(h0n.astype(BF16), win_ref[:, c0 + d_conv:c0 + 3 * d_conv], preferred_element_type=F32)
    pn = zn[:, 0:d_conv] * zn[:, d_conv:2 * d_conv]
    pn = jnp.where(i == pl.num_programs(0) - 1, 0.0, pn)
    pext = jnp.concatenate([phalo_s[...], p, pn], axis=0)
    yc = scw_ref[0:1, :] * pext[0:rows, :]
    for k in range(1, SHORT_CONV_W):
        yc = yc + scw_ref[k:k + 1, :] * pext[k * SUBLANES:k * SUBLANES + rows, :]
    phalo_s[...] = p[rows - SUBLANES:rows, :]
    y_conv = cb * yc

    ml = _rms_norm(y_lru, gl_ref[...]).astype(BF16)
    mc = _rms_norm(y_conv, gc_ref[...]).astype(BF16)
    o = (jnp.dot(ml, wout_ref[0:d_lru, :], preferred_element_type=F32)
         + jnp.dot(mc, wout_ref[d_lru:d_lru + d_conv, :], preferred_element_type=F32))
    o_ref[...] = _layer_norm(alpha * h0 + o, l1g_ref[...], l1b_ref[...])


def _top_rows(s, count):
    n = s.shape[0]
    iota = lax.broadcasted_iota(jnp.int32, s.shape, 0)
    vals, idxs = [], []
    for _ in range(count):
        m = jnp.max(s, axis=0, keepdims=True)
        sel = jnp.min(jnp.where(s == m, iota, n), axis=0, keepdims=True)
        vals.append(m)
        idxs.append(sel)
        s = jnp.where(iota == sel, -jnp.inf, s)
    return jnp.concatenate(vals, axis=0), jnp.concatenate(idxs, axis=0)


def _route_kernel(h_ref, wq_ref, k1_ref, k2_ref, idx_ref, g_ref, s1_s, s2_s, *, n_keys):
    groups = idx_ref.shape[0]
    half = k1_ref.shape[1]
    q = jnp.dot(h_ref[...].astype(BF16), wq_ref[...], preferred_element_type=F32).astype(BF16)
    nt = (((1,), (1,)), ((), ()))
    s1 = lax.dot_general(k1_ref[...], q[:, 0:half], nt, preferred_element_type=F32)
    s2 = lax.dot_general(k2_ref[...], q[:, half:2 * half], nt, preferred_element_type=F32)
    for c in range(groups):
        s1_s[c] = s1[:, c * LANES:(c + 1) * LANES]
        s2_s[c] = s2[:, c * LANES:(c + 1) * LANES]

    def group(c, carry):
        v1, i1 = _top_rows(s1_s[c], PEER_TOPK)
        v2, i2 = _top_rows(s2_s[c], PEER_TOPK)
        iota8 = lax.broadcasted_iota(jnp.int32, (SUBLANES, LANES), 0)
        cands, cand_es, poss = [], [], []
        for b in range(SUBLANES):
            for ab in range(PEER_TOPK // SUBLANES):
                if (ab * SUBLANES + 1) * (b + 1) <= PEER_TOPK:
                    rows = slice(ab * SUBLANES, (ab + 1) * SUBLANES)
                    cands.append(v1[rows, :] + v2[b:b + 1, :])
                    cand_es.append(i1[rows, :] * n_keys + i2[b:b + 1, :])
                    poss.append((iota8 + ab * SUBLANES) * PEER_TOPK + b)
        for a in range(PEER_TOPK):
            for bb in range(1, PEER_TOPK // SUBLANES):
                if (a + 1) * (bb * SUBLANES + 1) <= PEER_TOPK:
                    rows = slice(bb * SUBLANES, (bb + 1) * SUBLANES)
                    cands.append(v1[a:a + 1, :] + v2[rows, :])
                    cand_es.append(i1[a:a + 1, :] * n_keys + i2[rows, :])
                    poss.append(a * PEER_TOPK + bb * SUBLANES + iota8)
        cand = jnp.concatenate(cands, axis=0)
        cand_e = jnp.concatenate(cand_es, axis=0)
        cpos = jnp.concatenate(poss, axis=0)
        tops, ids = [], []
        for _ in range(PEER_TOPK):
            m = jnp.max(cand, axis=0, keepdims=True)
            pos = jnp.min(jnp.where(cand == m, cpos, PEER_TOPK * PEER_TOPK), axis=0, keepdims=True)
            hit = cpos == pos
            tops.append(m)
            ids.append(jnp.max(jnp.where(hit, cand_e, -1), axis=0, keepdims=True))
            cand = jnp.where(hit, -jnp.inf, cand)
        top = jnp.concatenate(tops, axis=0)
        ex = jnp.exp(top - top[0:1, :])
        g_ref[c] = ex / jnp.sum(ex, axis=0, keepdims=True)
        idx_ref[c] = jnp.concatenate(ids, axis=0) * HALF_ROWS
        return carry

    lax.fori_loop(0, groups, group, 0)


def _unpack(words):
    lo = lax.bitcast_convert_type(words << 16, F32)
    hi = lax.bitcast_convert_type(words & jnp.uint32(0xFFFF0000), F32)
    return lo, hi


def _u_pairs(idx_ref, x_ref, tab_ref, slot, t):
    xt = x_ref[t]
    x_lo = xt[0:HALF_ROWS, :]
    x_hi = xt[HALF_ROWS:2 * HALF_ROWS, :]
    for j in range(idx_ref.shape[2]):
        row = pl.multiple_of(idx_ref[0, t, j], HALF_ROWS)
        lo, hi = _unpack(tab_ref[pl.ds(row, HALF_ROWS), :])
        slot[pl.ds(j, HALF_ROWS, stride=SLOT_STRIDE), :] = lo * x_lo + hi * x_hi


def _u_dots(slot, n_pairs):
    part = slot[0:n_pairs, :]
    for s in range(1, HALF_ROWS):
        part = part + slot[s * SLOT_STRIDE:s * SLOT_STRIDE + n_pairs, :]
    return jnp.sum(part, axis=1, keepdims=True)


def _expert_u_kernel(idx_ref, x_ref, g_ref, tab_ref, w_ref, slot_a, slot_b):
    n_pairs = idx_ref.shape[2]
    lane = lax.broadcasted_iota(jnp.int32, (n_pairs, TOKEN_GROUP), 1)
    slot_b[...] = jnp.zeros_like(slot_b)

    def two_tokens(i, act):
        t0 = 2 * i
        _u_pairs(idx_ref, x_ref, tab_ref, slot_a, t0)
        act = jnp.where(lane == t0 - 1, _u_dots(slot_b, n_pairs), act)
        _u_pairs(idx_ref, x_ref, tab_ref, slot_b, t0 + 1)
        return jnp.where(lane == t0, _u_dots(slot_a, n_pairs), act)

    act = lax.fori_loop(0, TOKEN_GROUP // 2, two_tokens, jnp.zeros((n_pairs, TOKEN_GROUP), F32))
    act = jnp.where(lane == TOKEN_GROUP - 1, _u_dots(slot_b, n_pairs), act)
    w_ref[0] = g_ref[0] * _gelu(act)


def _expert_v_kernel(idx_ref, w_ref, x_ref, tab_ref, lng_ref, lnb_ref, o_ref, wb_a, wb_b, wsplit_s, *, alpha):
    n_pairs = idx_ref.shape[2]
    n_acc = 4
    w = w_ref[0]
    w_hi = w.astype(BF16)
    rest = w - w_hi.astype(F32)
    w_mid = rest.astype(BF16)
    wsplit_s[0] = w_hi
    wsplit_s[1] = w_mid
    wsplit_s[2] = (rest - w_mid.astype(F32)).astype(BF16)

    def spread(t):
        tok = lax.broadcasted_iota(jnp.int32, (TOKEN_GROUP, LANES), 0)
        sel = jnp.where(tok == t, 1.0, 0.0).astype(BF16)
        return ((jnp.dot(wsplit_s[0], sel, preferred_element_type=F32)
                 + jnp.dot(wsplit_s[1], sel, preferred_element_type=F32))
                + jnp.dot(wsplit_s[2], sel, preferred_element_type=F32))

    def pairs(t, wb):
        acc_lo = [jnp.zeros((HALF_ROWS, LANES), F32) for _ in range(n_acc)]
        acc_hi = [jnp.zeros((HALF_ROWS, LANES), F32) for _ in range(n_acc)]
        for j in range(n_pairs):
            row = pl.multiple_of(idx_ref[0, t, j], HALF_ROWS)
            wj = jnp.broadcast_to(wb[j:j + 1, :], (HALF_ROWS, LANES))
            lo, hi = _unpack(tab_ref[pl.ds(row, HALF_ROWS), :])
            acc_lo[j % n_acc] = acc_lo[j % n_acc] + wj * lo
            acc_hi[j % n_acc] = acc_hi[j % n_acc] + wj * hi
        y_lo = (acc_lo[0] + acc_lo[1]) + (acc_lo[2] + acc_lo[3])
        y_hi = (acc_hi[0] + acc_hi[1]) + (acc_hi[2] + acc_hi[3])
        o_ref[t] = alpha * x_ref[t] + jnp.concatenate([y_lo, y_hi], axis=0)

    wb_a[...] = spread(0)

    def two_tokens(i, carry):
        t0 = 2 * i
        wb_b[...] = spread(t0 + 1)
        pairs(t0, wb_a)
        wb_a[...] = spread(jnp.minimum(t0 + 2, TOKEN_GROUP - 1))
        pairs(t0 + 1, wb_b)
        return carry

    lax.fori_loop(0, TOKEN_GROUP // 2, two_tokens, 0)
    h = o_ref[...]
    d = h.shape[1] * h.shape[2]
    mu = jnp.sum(jnp.sum(h, axis=2, keepdims=True), axis=1, keepdims=True) / d
    hc = h - mu
    var = jnp.sum(jnp.sum(hc * hc, axis=2, keepdims=True), axis=1, keepdims=True) / d
    o_ref[...] = hc * lax.rsqrt(var + LN_EPS) * lng_ref[...] + lnb_ref[...]


def _pack_table(tab):
    e, d = tab.shape
    bits = lax.bitcast_convert_type(tab.astype(BF16), jnp.uint16).astype(jnp.uint32)
    bits = bits.reshape(e, 2, d // 2)
    words = bits[:, 0, :] | (bits[:, 1, :] << 16)
    return words.reshape(e * (d // 2 // LANES), LANES)


def _const_spec(shape):
    zeros = (0,) * len(shape)
    return pl.BlockSpec(shape, lambda *_: zeros, pipeline_mode=pl.Buffered(1))


def _row(v):
    return v.reshape(1, -1).astype(F32)


def _params(semantics):
    return pltpu.CompilerParams(dimension_semantics=semantics, vmem_limit_bytes=VMEM_LIMIT_BYTES)


def kernel(x, ln_emb_g, ln_emb_b, w_in, lru_conv_w, lru_conv_b, lru_wa, lru_ba, lru_wx, lru_bx, lru_lambda, conv_w, norm_lru_g, norm_conv_g, w_out, ln1_g, ln1_b, peer_wq, peer_k1, peer_k2, peer_u, peer_v, ln2_g, ln2_b):
    batch, seq, d_model = x.shape
    depth = w_in.shape[0]
    assert batch == SUBLANES and d_model == SUBLANES * LANES
    assert depth == 1
    d_lru = lru_conv_w.shape[-1]
    d_conv = conv_w.shape[-1]
    n_keys, half = peer_k1.shape[1], peer_k1.shape[2]
    n_tokens = batch * seq
    steps = min(STEPS_PER_CHUNK, seq)
    rows = steps * batch
    n_chunks = seq // steps
    assert seq % steps == 0 and steps % 8 == 0
    route_tokens = min(ROUTE_TOKENS, n_tokens)
    assert n_tokens % route_tokens == 0 and route_tokens % TOKEN_GROUP == 0
    n_groups = n_tokens // TOKEN_GROUP
    halo_rows = (LRU_CONV_W - 1) * batch
    alpha = (2.0 * depth) ** 0.25

    h = x.transpose(1, 0, 2).reshape(n_tokens, d_model)
    ln_g, ln_b = _row(ln_emb_g), _row(ln_emb_b)

    for l in range(depth):
        w_in_b = w_in[l].astype(BF16)
        wa_b, wx_b = lru_wa[l].astype(BF16), lru_wx[l].astype(BF16)
        lru_small = lambda d: [lru_conv_w[l, d], _row(lru_conv_b[l, d]), wa_b[d], _row(lru_ba[l, d]),
                               wx_b[d], _row(lru_bx[l, d]), _row(lru_lambda[l, d])]
        lru_specs = [_const_spec((LRU_CONV_W, d_lru)), _const_spec((1, d_lru)),
                     _const_spec(wa_b.shape[1:]), _const_spec((1, d_lru)),
                     _const_spec(wx_b.shape[1:]), _const_spec((1, d_lru)), _const_spec((1, d_lru))]
        lru_scratch = [pltpu.VMEM((rows, d_lru), F32), pltpu.VMEM((rows, d_lru), F32),
                       pltpu.VMEM((batch, d_lru), F32), pltpu.VMEM((halo_rows, d_lru), F32)]

        hb = pl.pallas_call(
            _lru_bwd_kernel,
            grid=(n_chunks,),
            in_specs=[pl.BlockSpec((rows, d_model), lambda i: (n_chunks - 1 - i, 0)),
                      _const_spec((1, d_model)), _const_spec((1, d_model)),
                      _const_spec((d_model, d_lru))] + lru_specs,
            out_specs=pl.BlockSpec((rows, d_lru), lambda i: (n_chunks - 1 - i, 0)),
            out_shape=jax.ShapeDtypeStruct((n_tokens, d_lru), F32),
            scratch_shapes=lru_scratch,
            compiler_params=_params(("arbitrary",)),
            name="lru_bwd",
        )(h, ln_g, ln_b, w_in_b[:, 0:d_lru], *lru_small(1))

        h1 = pl.pallas_call(
            functools.partial(_mixer_kernel, alpha=alpha, d_lru=d_lru, d_conv=d_conv),
            grid=(n_chunks,),
            in_specs=[pl.BlockSpec((rows, d_model), lambda i: (i, 0)),
                      pl.BlockSpec((batch, d_model), lambda i: (jnp.minimum((i + 1) * steps, seq - 1), 0)),
                      pl.BlockSpec((rows, d_lru), lambda i: (i, 0)),
                      _const_spec((1, d_model)), _const_spec((1, d_model)),
                      _const_spec(w_in_b.shape)] + lru_specs +
                     [_const_spec((SHORT_CONV_W, d_conv)), _const_spec((1, d_lru)), _const_spec((1, d_conv)),
                      _const_spec(w_out.shape[1:]), _const_spec((1, d_model)), _const_spec((1, d_model))],
            out_specs=pl.BlockSpec((rows, d_model), lambda i: (i, 0)),
            out_shape=jax.ShapeDtypeStruct((n_tokens, d_model), F32),
            scratch_shapes=lru_scratch + [pltpu.VMEM((batch, d_conv), F32)],
            compiler_params=_params(("arbitrary",)),
            name="mixer",
        )(h, h, hb, ln_g, ln_b, w_in_b, *lru_small(0), conv_w[l], _row(norm_lru_g[l]), _row(norm_conv_g[l]),
          w_out[l].astype(BF16), _row(ln1_g[l]), _row(ln1_b[l]))

        groups_per_step = route_tokens // TOKEN_GROUP
        n_pairs = PEER_HEADS * PEER_TOPK
        idx3, g3 = pl.pallas_call(
            functools.partial(_route_kernel, n_keys=n_keys),
            grid=(n_tokens // route_tokens, PEER_HEADS),
            in_specs=[pl.BlockSpec((route_tokens, d_model), lambda i, hd: (i, 0)),
                      pl.BlockSpec((d_model, 2 * half), lambda i, hd: (0, hd)),
                      _const_spec((n_keys, half)), _const_spec((n_keys, half))],
            out_specs=[pl.BlockSpec((groups_per_step, PEER_TOPK, TOKEN_GROUP), lambda i, hd: (i, hd, 0)),
                       pl.BlockSpec((groups_per_step, PEER_TOPK, TOKEN_GROUP), lambda i, hd: (i, hd, 0))],
            out_shape=[jax.ShapeDtypeStruct((n_groups, n_pairs, TOKEN_GROUP), jnp.int32),
                       jax.ShapeDtypeStruct((n_groups, n_pairs, TOKEN_GROUP), F32)],
            scratch_shapes=[pltpu.VMEM((groups_per_step, n_keys, LANES), F32),
                            pltpu.VMEM((groups_per_step, n_keys, LANES), F32)],
            compiler_params=_params(("arbitrary", "arbitrary")),
            name="route",
        )(h1, peer_wq[l].astype(BF16), peer_k1[l].astype(BF16), peer_k2[l].astype(BF16))

        u_tab = _pack_table(peer_u[l])
        v_tab = _pack_table(peer_v[l])
        h1_3 = h1.reshape(n_tokens, SUBLANES, LANES)
        idx3_t = idx3.swapaxes(1, 2)
        assert n_pairs == TOKEN_GROUP
        group_spec = lambda space: pl.BlockSpec((1, n_pairs, TOKEN_GROUP), lambda i: (i, 0, 0), memory_space=space)
        tok_spec = pl.BlockSpec((TOKEN_GROUP, SUBLANES, LANES), lambda i: (i, 0, 0))

        w3 = pl.pallas_call(
            _expert_u_kernel,
            grid=(n_groups,),
            in_specs=[group_spec(pltpu.SMEM), tok_spec, group_spec(pltpu.VMEM), _const_spec(u_tab.shape)],
            out_specs=group_spec(pltpu.VMEM),
            out_shape=jax.ShapeDtypeStruct((n_groups, n_pairs, TOKEN_GROUP), F32),
            scratch_shapes=[pltpu.VMEM((HALF_ROWS * SLOT_STRIDE, LANES), F32),
                            pltpu.VMEM((HALF_ROWS * SLOT_STRIDE, LANES), F32)],
            compiler_params=_params(("arbitrary",)),
            name="expert_u",
        )(idx3_t, h1_3, g3, u_tab)

        h2 = pl.pallas_call(
            functools.partial(_expert_v_kernel, alpha=alpha),
            grid=(n_groups,),
            in_specs=[group_spec(pltpu.SMEM), group_spec(pltpu.VMEM), tok_spec, _const_spec(v_tab.shape),
                      _const_spec((SUBLANES, LANES)), _const_spec((SUBLANES, LANES))],
            out_specs=tok_spec,
            out_shape=jax.ShapeDtypeStruct((n_tokens, SUBLANES, LANES), F32),
            scratch_shapes=[pltpu.VMEM((n_pairs, LANES), F32), pltpu.VMEM((n_pairs, LANES), F32),
                            pltpu.VMEM((3, n_pairs, TOKEN_GROUP), BF16)],
            compiler_params=_params(("arbitrary",)),
            name="expert_v",
        )(idx3_t, w3, h1_3, v_tab, ln2_g[l].reshape(SUBLANES, LANES), ln2_b[l].reshape(SUBLANES, LANES))

        h = h2.reshape(n_tokens, d_model)

    return h.reshape(seq, batch, d_model).transpose(1, 0, 2)
```

```python
import functools

import jax
import jax.numpy as jnp
from jax import lax
from jax.experimental import pallas as pl
from jax.experimental.pallas import tpu as pltpu

F32 = jnp.float32
BF16 = jnp.bfloat16

SUBLANES = 8
LANES = 128
VMEM_LIMIT_BYTES = 56 * 1024 * 1024

LRU_HEADS = 8
LRU_CONV_W = 4
LRU_C = 8.0
SHORT_CONV_W = 3
PEER_HEADS = 8
PEER_TOPK = 16
LN_EPS = 1e-5
RMS_EPS = 1e-6
SQRT_HALF = 0.7071067811865476

STEPS_PER_CHUNK = 32
ROUTE_TOKENS = 512
TOKEN_GROUP = 128
PER_PHASE = 2
HALF_ROWS = 4
SLOT_STRIDE = TOKEN_GROUP // 2 + HALF_ROWS


def _layer_norm(x, g, b):
    mu = jnp.mean(x, axis=-1, keepdims=True)
    xc = x - mu
    var = jnp.mean(xc * xc, axis=-1, keepdims=True)
    return xc * lax.rsqrt(var + LN_EPS) * g + b


def _rms_norm(x, g):
    return x * lax.rsqrt(jnp.mean(x * x, axis=-1, keepdims=True) + RMS_EPS) * g


def _gelu(x):
    return 0.5 * x * (1.0 + lax.erf(x * SQRT_HALF))


def _softplus(y):
    return jnp.maximum(y, 0.0) + jnp.log1p(jnp.exp(-jnp.abs(y)))


def _lru_coeffs(xc, wa_ref, ba_ref, wx_ref, bx_ref, lam_ref):
    blk = wa_ref.shape[-1]
    xcb = xc.astype(BF16)
    rs, gs = [], []
    for hd in range(LRU_HEADS):
        xh = xcb[:, hd * blk:(hd + 1) * blk]
        rs.append(jnp.dot(xh, wa_ref[hd], preferred_element_type=F32))
        gs.append(jnp.dot(xh, wx_ref[hd], preferred_element_type=F32))
    r = jax.nn.sigmoid(jnp.concatenate(rs, axis=1) + ba_ref[...])
    gate_i = jax.nn.sigmoid(jnp.concatenate(gs, axis=1) + bx_ref[...])
    log_a = -LRU_C * r * _softplus(-lam_ref[...])
    a = jnp.exp(log_a)
    y = 2.0 * log_a
    e = a * a
    neg_expm1 = jnp.where(e == 1.0, -y, (1.0 - e) * y / jnp.log(e))
    u = jnp.sqrt(neg_expm1) * (gate_i * xc)
    return a, u


def _scan_chunk(a_s, u_s, h_s, steps, reverse):
    unroll = 8

    def body(k, h):
        for q in range(unroll):
            step = k * unroll + q
            if reverse:
                step = steps - 1 - step
            row = pl.multiple_of(step * SUBLANES, SUBLANES)
            h = a_s[pl.ds(row, SUBLANES), :] * h + u_s[pl.ds(row, SUBLANES), :]
            u_s[pl.ds(row, SUBLANES), :] = h
        return h

    h_s[...] = lax.fori_loop(0, steps // unroll, body, h_s[...])


def _lru_bwd_kernel(x_ref, lng_ref, lnb_ref, wxr_ref, cw_ref, cb_ref, wa_ref, ba_ref, wx_ref, bx_ref,
                    lam_ref, hb_ref, a_s, u_s, h_s, halo_s):
    rows = x_ref.shape[0]
    halo_rows = halo_s.shape[0]

    @pl.when(pl.program_id(0) == 0)
    def _():
        h_s[...] = jnp.zeros_like(h_s)
        halo_s[...] = jnp.zeros_like(halo_s)

    h0 = _layer_norm(x_ref[...], lng_ref[...], lnb_ref[...])
    xr = jnp.dot(h0.astype(BF16), wxr_ref[...], preferred_element_type=F32)
    ext = jnp.concatenate([xr, halo_s[...]], axis=0)
    xc = cb_ref[...]
    for k in range(LRU_CONV_W):
        xc = xc + cw_ref[k:k + 1, :] * ext[k * SUBLANES:k * SUBLANES + rows, :]
    halo_s[...] = xr[0:halo_rows, :]
    a, u = _lru_coeffs(xc, wa_ref, ba_ref, wx_ref, bx_ref, lam_ref)
    a_s[...] = a
    u_s[...] = u
    _scan_chunk(a_s, u_s, h_s, rows // SUBLANES, reverse=True)
    hb_ref[...] = u_s[...]


def _mixer_kernel(x_ref, xn_ref, hb_ref, lng_ref, lnb_ref, win_ref, cw_ref, cb_ref, wa_ref, ba_ref,
                  wx_ref, bx_ref, lam_ref, scw_ref, gl_ref, gc_ref, wout_ref, l1g_ref, l1b_ref,
                  o_ref, a_s, u_s, h_s, halo_s, phalo_s, *, alpha, d_lru, d_conv):
    rows = x_ref.shape[0]
    halo_rows = halo_s.shape[0]
    i = pl.program_id(0)

    @pl.when(i == 0)
    def _():
        h_s[...] = jnp.zeros_like(h_s)
        halo_s[...] = jnp.zeros_like(halo_s)
        phalo_s[...] = jnp.zeros_like(phalo_s)

    h0 = _layer_norm(x_ref[...], lng_ref[...], lnb_ref[...])
    z = jnp.dot(h0.astype(BF16), win_ref[...], preferred_element_type=F32)
    xr = z[:, 0:d_lru]
    gate = z[:, d_lru:2 * d_lru]
    c0 = 2 * d_lru
    cb = z[:, c0:c0 + d_conv]
    cc = z[:, c0 + d_conv:c0 + 2 * d_conv]
    cx = z[:, c0 + 2 * d_conv:c0 + 3 * d_conv]

    ext = jnp.concatenate([halo_s[...], xr], axis=0)
    xc = cb_ref[...]
    for k in range(LRU_CONV_W):
        xc = xc + cw_ref[k:k + 1, :] * ext[k * SUBLANES:k * SUBLANES + rows, :]
    halo_s[...] = xr[rows - halo_rows:rows, :]
    a, u = _lru_coeffs(xc, wa_ref, ba_ref, wx_ref, bx_ref, lam_ref)
    a_s[...] = a
    u_s[...] = u
    _scan_chunk(a_s, u_s, h_s, rows // SUBLANES, reverse=False)
    y_lru = _gelu(gate) * (u_s[...] + hb_ref[...])

    p = cc * cx
    h0n = _layer_norm(xn_ref[...], lng_ref[...], lnb_ref[...])
    zn = jnp.dot(h0n.astype(BF16), win_ref[:, c0 + d_conv:c0 + 3 * d_conv], preferred_element_type=F32)
    pn = zn[:, 0:d_conv] * zn[:, d_conv:2 * d_conv]
    pn = jnp.where(i == pl.num_programs(0) - 1, 0.0, pn)
    pext = jnp.concatenate([phalo_s[...], p, pn], axis=0)
    yc = scw_ref[0:1, :] * pext[0:rows, :]
    for k in range(1, SHORT_CONV_W):
        yc = yc + scw_ref[k:k + 1, :] * pext[k * SUBLANES:k * SUBLANES + rows, :]
    phalo_s[...] = p[rows - SUBLANES:rows, :]
    y_conv = cb * yc

    ml = _rms_norm(y_lru, gl_ref[...]).astype(BF16)
    mc = _rms_norm(y_conv, gc_ref[...]).astype(BF16)
    o = (jnp.dot(ml, wout_ref[0:d_lru, :], preferred_element_type=F32)
         + jnp.dot(mc, wout_ref[d_lru:d_lru + d_conv, :], preferred_element_type=F32))
    o_ref[...] = _layer_norm(alpha * h0 + o, l1g_ref[...], l1b_ref[...])


def _top_rows(s, count):
    n = s.shape[0]
    iota = lax.broadcasted_iota(jnp.int32, s.shape, 0)
    vals, idxs = [], []
    for _ in range(count):
        m = jnp.max(s, axis=0, keepdims=True)
        sel = jnp.min(jnp.where(s == m, iota, n), axis=0, keepdims=True)
        vals.append(m)
        idxs.append(sel)
        s = jnp.where(iota == sel, -jnp.inf, s)
    return jnp.concatenate(vals, axis=0), jnp.concatenate(idxs, axis=0)


def _route_kernel(h_ref, wq_ref, k1_ref, k2_ref, idx_ref, g_ref, s1_s, s2_s, *, n_keys):
    groups = idx_ref.shape[0]
    half = k1_ref.shape[1]
    q = jnp.dot(h_ref[...].astype(BF16), wq_ref[...], preferred_element_type=F32).astype(BF16)
    nt = (((1,), (1,)), ((), ()))
    s1 = lax.dot_general(k1_ref[...], q[:, 0:half], nt, preferred_element_type=F32)
    s2 = lax.dot_general(k2_ref[...], q[:, half:2 * half], nt, preferred_element_type=F32)
    for c in range(groups):
        s1_s[c] = s1[:, c * LANES:(c + 1) * LANES]
        s2_s[c] = s2[:, c * LANES:(c + 1) * LANES]

    def group(c, carry):
        v1, i1 = _top_rows(s1_s[c], PEER_TOPK)
        v2, i2 = _top_rows(s2_s[c], PEER_TOPK)
        iota8 = lax.broadcasted_iota(jnp.int32, (SUBLANES, LANES), 0)
        cands, cand_es, poss = [], [], []
        for b in range(SUBLANES):
            for ab in range(PEER_TOPK // SUBLANES):
                if (ab * SUBLANES + 1) * (b + 1) <= PEER_TOPK:
                    rows = slice(ab * SUBLANES, (ab + 1) * SUBLANES)
                    cands.append(v1[rows, :] + v2[b:b + 1, :])
                    cand_es.append(i1[rows, :] * n_keys + i2[b:b + 1, :])
                    poss.append((iota8 + ab * SUBLANES) * PEER_TOPK + b)
        for a in range(PEER_TOPK):
            for bb in range(1, PEER_TOPK // SUBLANES):
                if (a + 1) * (bb * SUBLANES + 1) <= PEER_TOPK:
                    rows = slice(bb * SUBLANES, (bb + 1) * SUBLANES)
                    cands.append(v1[a:a + 1, :] + v2[rows, :])
                    cand_es.append(i1[a:a + 1, :] * n_keys + i2[rows, :])
                    poss.append(a * PEER_TOPK + bb * SUBLANES + iota8)
        cand = jnp.concatenate(cands, axis=0)
        cand_e = jnp.concatenate(cand_es, axis=0)
        cpos = jnp.concatenate(poss, axis=0)
        tops, ids = [], []
        for _ in range(PEER_TOPK):
            m = jnp.max(cand, axis=0, keepdims=True)
            pos = jnp.min(jnp.where(cand == m, cpos, PEER_TOPK * PEER_TOPK), axis=0, keepdims=True)
            hit = cpos == pos
            tops.append(m)
            ids.append(jnp.max(jnp.where(hit, cand_e, -1), axis=0, keepdims=True))
            cand = jnp.where(hit, -jnp.inf, cand)
        top = jnp.concatenate(tops, axis=0)
        ex = jnp.exp(top - top[0:1, :])
        g_ref[c] = ex / jnp.sum(ex, axis=0, keepdims=True)
        lead = jnp.where(pl.program_id(1) < PEER_HEADS // 2, HALF_ROWS, 0)
        idx_ref[c] = jnp.concatenate(ids, axis=0) * HALF_ROWS + lead
        return carry

    lax.fori_loop(0, groups, group, 0)


def _unpack(words):
    lo = lax.bitcast_convert_type(words << 16, F32)
    hi = lax.bitcast_convert_type(words & jnp.uint32(0xFFFF0000), F32)
    return lo, hi


def _couple_rows(idx_ref, offs, t):
    n = len(offs)
    half = idx_ref.shape[2] // 2
    for m in range(half // n):
        view_a = idx_ref.at[0, t, pl.ds(n * m, n)]
        view_b = idx_ref.at[0, t, pl.ds(half + n * m, n)]
        for q in range(n):
            yield (n * m + q, pl.multiple_of(view_a[offs[q]], HALF_ROWS),
                   pl.multiple_of(view_b[offs[q]], HALF_ROWS))


def _gather_couple(tab_ref, upper, row_a, row_b):
    return jnp.where(upper, tab_ref[pl.ds(row_b, SUBLANES), :], tab_ref[pl.ds(row_a, SUBLANES), :])


def _u_couples(idx_ref, offs, x_ref, tab_ref, slot, upper, t):
    xt = x_ref[t]
    x_lo = jnp.concatenate([xt[0:HALF_ROWS, :]] * 2, axis=0)
    x_hi = jnp.concatenate([xt[HALF_ROWS:SUBLANES, :]] * 2, axis=0)
    for c, row_a, row_b in _couple_rows(idx_ref, offs, t):
        lo, hi = _unpack(_gather_couple(tab_ref, upper, row_a, row_b))
        slot[pl.ds(c, SUBLANES, stride=SLOT_STRIDE), :] = lo * x_lo + hi * x_hi


def _u_dots(slot, half):
    slabs = [slot[s * SLOT_STRIDE:s * SLOT_STRIDE + half, :] for s in range(SUBLANES)]
    part_a = (slabs[0] + slabs[1]) + (slabs[2] + slabs[3])
    part_b = (slabs[4] + slabs[5]) + (slabs[6] + slabs[7])
    return jnp.sum(jnp.concatenate([part_a, part_b], axis=0), axis=1, keepdims=True)


def _expert_u_kernel(offs_ref, idx_ref, x_ref, g_ref, tab_ref, w_ref, slot_a, slot_b):
    n_pairs = idx_ref.shape[2]
    offs = [offs_ref[q] for q in range(offs_ref.shape[0])]
    lane = lax.broadcasted_iota(jnp.int32, (n_pairs, TOKEN_GROUP), 1)
    upper = lax.broadcasted_iota(jnp.int32, (SUBLANES, LANES), 0) >= HALF_ROWS
    slot_b[...] = jnp.zeros_like(slot_b)

    def fill(slot, t0):
        for k in range(PER_PHASE):
            _u_couples(idx_ref, offs, x_ref, tab_ref, slot.at[k], upper, t0 + k)

    def reduce(slot, t0, act):
        for k in range(PER_PHASE):
            act = jnp.where(lane == t0 + k, _u_dots(slot.at[k], n_pairs // 2), act)
        return act

    def step(i, act):
        t0 = 2 * PER_PHASE * i
        fill(slot_a, t0)
        act = reduce(slot_b, t0 - PER_PHASE, act)
        fill(slot_b, t0 + PER_PHASE)
        return reduce(slot_a, t0, act)

    act = lax.fori_loop(0, TOKEN_GROUP // (2 * PER_PHASE), step, jnp.zeros((n_pairs, TOKEN_GROUP), F32))
    act = reduce(slot_b, TOKEN_GROUP - PER_PHASE, act)
    w_ref[0] = g_ref[0] * _gelu(act)


def _expert_v_kernel(offs_ref, idx_ref, w_ref, x_ref, tab_ref, lng_ref, lnb_ref, o_ref,
                     wb_a, wb_b, wsplit_s, onehot_s, *, alpha):
    n_pairs = idx_ref.shape[2]
    half = n_pairs // 2
    offs = [offs_ref[q] for q in range(offs_ref.shape[0])]
    upper = lax.broadcasted_iota(jnp.int32, (SUBLANES, LANES), 0) >= HALF_ROWS
    n_acc = 2

    w = w_ref[0]
    w_hi = w.astype(BF16)
    rest = w - w_hi.astype(F32)
    w_mid = rest.astype(BF16)
    wsplit_s[0] = w_hi
    wsplit_s[1] = w_mid
    wsplit_s[2] = (rest - w_mid.astype(F32)).astype(BF16)
    onehot_s[...] = jnp.where(
        lax.broadcasted_iota(jnp.int32, onehot_s.shape, 0) == TOKEN_GROUP, 1.0, 0.0)

    def spread(t):
        sel = onehot_s[pl.ds(TOKEN_GROUP - t, TOKEN_GROUP), :].astype(BF16)
        return ((jnp.dot(wsplit_s[0], sel, preferred_element_type=F32)
                 + jnp.dot(wsplit_s[1], sel, preferred_element_type=F32))
                + jnp.dot(wsplit_s[2], sel, preferred_element_type=F32))

    def pairs(t, wb):
        acc_lo = [jnp.zeros((SUBLANES, LANES), F32) for _ in range(n_acc)]
        acc_hi = [jnp.zeros((SUBLANES, LANES), F32) for _ in range(n_acc)]
        for c, row_a, row_b in _couple_rows(idx_ref, offs, t):
            wc = jnp.where(upper, jnp.broadcast_to(wb[half + c:half + c + 1, :], (SUBLANES, LANES)),
                           jnp.broadcast_to(wb[c:c + 1, :], (SUBLANES, LANES)))
            lo, hi = _unpack(_gather_couple(tab_ref, upper, row_a, row_b))
            acc_lo[c % n_acc] = acc_lo[c % n_acc] + wc * lo
            acc_hi[c % n_acc] = acc_hi[c % n_acc] + wc * hi
        y_lo = acc_lo[0] + acc_lo[1]
        y_hi = acc_hi[0] + acc_hi[1]
        y = jnp.concatenate([y_lo[0:HALF_ROWS, :] + y_lo[HALF_ROWS:SUBLANES, :],
                             y_hi[0:HALF_ROWS, :] + y_hi[HALF_ROWS:SUBLANES, :]], axis=0)
        o_ref[t] = alpha * x_ref[t] + y

    def spread_into(wb, t0):
        for k in range(PER_PHASE):
            wb[k] = spread(jnp.minimum(t0 + k, TOKEN_GROUP - 1))

    def accumulate(wb, t0):
        for k in range(PER_PHASE):
            pairs(t0 + k, wb.at[k])

    spread_into(wb_a, 0)

    def step(i, carry):
        t0 = 2 * PER_PHASE * i
        spread_into(wb_b, t0 + PER_PHASE)
        accumulate(wb_a, t0)
        spread_into(wb_a, t0 + 2 * PER_PHASE)
        accumulate(wb_b, t0 + PER_PHASE)
        return carry

    lax.fori_loop(0, TOKEN_GROUP // (2 * PER_PHASE), step, 0)
    h = o_ref[...]
    d = h.shape[1] * h.shape[2]
    mu = jnp.sum(jnp.sum(h, axis=2, keepdims=True), axis=1, keepdims=True) / d
    hc = h - mu
    var = jnp.sum(jnp.sum(hc * hc, axis=2, keepdims=True), axis=1, keepdims=True) / d
    o_ref[...] = hc * lax.rsqrt(var + LN_EPS) * lng_ref[...] + lnb_ref[...]


def _pack_table(tab):
    e, d = tab.shape
    bits = lax.bitcast_convert_type(tab.astype(BF16), jnp.uint16).astype(jnp.uint32)
    bits = bits.reshape(e, 2, d // 2)
    words = bits[:, 0, :] | (bits[:, 1, :] << 16)
    words = words.reshape(e * (d // 2 // LANES), LANES)
    return jnp.pad(words, ((HALF_ROWS, HALF_ROWS), (0, 0)))


def _const_spec(shape):
    zeros = (0,) * len(shape)
    return pl.BlockSpec(shape, lambda *_: zeros, pipeline_mode=pl.Buffered(1))


def _row(v):
    return v.reshape(1, -1).astype(F32)


def _params(semantics):
    return pltpu.CompilerParams(dimension_semantics=semantics, vmem_limit_bytes=VMEM_LIMIT_BYTES)


def kernel(x, ln_emb_g, ln_emb_b, w_in, lru_conv_w, lru_conv_b, lru_wa, lru_ba, lru_wx, lru_bx, lru_lambda, conv_w, norm_lru_g, norm_conv_g, w_out, ln1_g, ln1_b, peer_wq, peer_k1, peer_k2, peer_u, peer_v, ln2_g, ln2_b):
    batch, seq, d_model = x.shape
    depth = w_in.shape[0]
    assert batch == SUBLANES and d_model == SUBLANES * LANES
    assert depth == 1
    d_lru = lru_conv_w.shape[-1]
    d_conv = conv_w.shape[-1]
    n_keys, half = peer_k1.shape[1], peer_k1.shape[2]
    n_tokens = batch * seq
    steps = min(STEPS_PER_CHUNK, seq)
    rows = steps * batch
    n_chunks = seq // steps
    assert seq % steps == 0 and steps % 8 == 0
    route_tokens = min(ROUTE_TOKENS, n_tokens)
    assert n_tokens % route_tokens == 0 and route_tokens % TOKEN_GROUP == 0
    n_groups = n_tokens // TOKEN_GROUP
    halo_rows = (LRU_CONV_W - 1) * batch
    alpha = (2.0 * depth) ** 0.25

    h = x.transpose(1, 0, 2).reshape(n_tokens, d_model)
    ln_g, ln_b = _row(ln_emb_g), _row(ln_emb_b)

    for l in range(depth):
        w_in_b = w_in[l].astype(BF16)
        wa_b, wx_b = lru_wa[l].astype(BF16), lru_wx[l].astype(BF16)
        lru_small = lambda d: [lru_conv_w[l, d], _row(lru_conv_b[l, d]), wa_b[d], _row(lru_ba[l, d]),
                               wx_b[d], _row(lru_bx[l, d]), _row(lru_lambda[l, d])]
        lru_specs = [_const_spec((LRU_CONV_W, d_lru)), _const_spec((1, d_lru)),
                     _const_spec(wa_b.shape[1:]), _const_spec((1, d_lru)),
                     _const_spec(wx_b.shape[1:]), _const_spec((1, d_lru)), _const_spec((1, d_lru))]
        lru_scratch = [pltpu.VMEM((rows, d_lru), F32), pltpu.VMEM((rows, d_lru), F32),
                       pltpu.VMEM((batch, d_lru), F32), pltpu.VMEM((halo_rows, d_lru), F32)]

        hb = pl.pallas_call(
            _lru_bwd_kernel,
            grid=(n_chunks,),
            in_specs=[pl.BlockSpec((rows, d_model), lambda i: (n_chunks - 1 - i, 0)),
                      _const_spec((1, d_model)), _const_spec((1, d_model)),
                      _const_spec((d_model, d_lru))] + lru_specs,
            out_specs=pl.BlockSpec((rows, d_lru), lambda i: (n_chunks - 1 - i, 0)),
            out_shape=jax.ShapeDtypeStruct((n_tokens, d_lru), F32),
            scratch_shapes=lru_scratch,
            compiler_params=_params(("arbitrary",)),
            name="lru_bwd",
        )(h, ln_g, ln_b, w_in_b[:, 0:d_lru], *lru_small(1))

        h1 = pl.pallas_call(
            functools.partial(_mixer_kernel, alpha=alpha, d_lru=d_lru, d_conv=d_conv),
            grid=(n_chunks,),
            in_specs=[pl.BlockSpec((rows, d_model), lambda i: (i, 0)),
                      pl.BlockSpec((batch, d_model), lambda i: (jnp.minimum((i + 1) * steps, seq - 1), 0)),
                      pl.BlockSpec((rows, d_lru), lambda i: (i, 0)),
                      _const_spec((1, d_model)), _const_spec((1, d_model)),
                      _const_spec(w_in_b.shape)] + lru_specs +
                     [_const_spec((SHORT_CONV_W, d_conv)), _const_spec((1, d_lru)), _const_spec((1, d_conv)),
                      _const_spec(w_out.shape[1:]), _const_spec((1, d_model)), _const_spec((1, d_model))],
            out_specs=pl.BlockSpec((rows, d_model), lambda i: (i, 0)),
            out_shape=jax.ShapeDtypeStruct((n_tokens, d_model), F32),
            scratch_shapes=lru_scratch + [pltpu.VMEM((batch, d_conv), F32)],
            compiler_params=_params(("arbitrary",)),
            name="mixer",
        )(h, h, hb, ln_g, ln_b, w_in_b, *lru_small(0), conv_w[l], _row(norm_lru_g[l]), _row(norm_conv_g[l]),
          w_out[l].astype(BF16), _row(ln1_g[l]), _row(ln1_b[l]))

        groups_per_step = route_tokens // TOKEN_GROUP
        n_pairs = PEER_HEADS * PEER_TOPK
        idx3, g3 = pl.pallas_call(
            functools.partial(_route_kernel, n_keys=n_keys),
            grid=(n_tokens // route_tokens, PEER_HEADS),
            in_specs=[pl.BlockSpec((route_tokens, d_model), lambda i, hd: (i, 0)),
                      pl.BlockSpec((d_model, 2 * half), lambda i, hd: (0, hd)),
                      _const_spec((n_keys, half)), _const_spec((n_keys, half))],
            out_specs=[pl.BlockSpec((groups_per_step, PEER_TOPK, TOKEN_GROUP), lambda i, hd: (i, hd, 0)),
                       pl.BlockSpec((groups_per_step, PEER_TOPK, TOKEN_GROUP), lambda i, hd: (i, hd, 0))],
            out_shape=[jax.ShapeDtypeStruct((n_groups, n_pairs, TOKEN_GROUP), jnp.int32),
                       jax.ShapeDtypeStruct((n_groups, n_pairs, TOKEN_GROUP), F32)],
            scratch_shapes=[pltpu.VMEM((groups_per_step, n_keys, LANES), F32),
                            pltpu.VMEM((groups_per_step, n_keys, LANES), F32)],
            compiler_params=_params(("arbitrary", "arbitrary")),
            name="route",
        )(h1, peer_wq[l].astype(BF16), peer_k1[l].astype(BF16), peer_k2[l].astype(BF16))

        u_tab = _pack_table(peer_u[l])
        v_tab = _pack_table(peer_v[l])
        h1_3 = h1.reshape(n_tokens, SUBLANES, LANES)
        idx3_t = idx3.swapaxes(1, 2)
        assert n_pairs == TOKEN_GROUP
        group_spec = lambda space: pl.BlockSpec((1, n_pairs, TOKEN_GROUP), lambda i: (i, 0, 0), memory_space=space)
        tok_spec = pl.BlockSpec((TOKEN_GROUP, SUBLANES, LANES), lambda i: (i, 0, 0))
        offs = jnp.arange(SUBLANES, dtype=jnp.int32)
        offs_spec = pl.BlockSpec(memory_space=pltpu.SMEM)

        w3 = pl.pallas_call(
            _expert_u_kernel,
            grid=(n_groups,),
            in_specs=[offs_spec, group_spec(pltpu.SMEM), tok_spec, group_spec(pltpu.VMEM), _const_spec(u_tab.shape)],
            out_specs=group_spec(pltpu.VMEM),
            out_shape=jax.ShapeDtypeStruct((n_groups, n_pairs, TOKEN_GROUP), F32),
            scratch_shapes=[pltpu.VMEM((PER_PHASE, SUBLANES * SLOT_STRIDE, LANES), F32),
                            pltpu.VMEM((PER_PHASE, SUBLANES * SLOT_STRIDE, LANES), F32)],
            compiler_params=_params(("arbitrary",)),
            name="expert_u",
        )(offs, idx3_t, h1_3, g3, u_tab)

        h2 = pl.pallas_call(
            functools.partial(_expert_v_kernel, alpha=alpha),
            grid=(n_groups,),
            in_specs=[offs_spec, group_spec(pltpu.SMEM), group_spec(pltpu.VMEM), tok_spec, _const_spec(v_tab.shape),
                      _const_spec((SUBLANES, LANES)), _const_spec((SUBLANES, LANES))],
            out_specs=tok_spec,
            out_shape=jax.ShapeDtypeStruct((n_tokens, SUBLANES, LANES), F32),
            scratch_shapes=[pltpu.VMEM((PER_PHASE, n_pairs, LANES), F32),
                            pltpu.VMEM((PER_PHASE, n_pairs, LANES), F32),
                            pltpu.VMEM((3, n_pairs, TOKEN_GROUP), BF16),
                            pltpu.VMEM((2 * TOKEN_GROUP, LANES), F32)],
            compiler_params=_params(("arbitrary",)),
            name="expert_v",
        )(offs, idx3_t, w3, h1_3, v_tab, ln2_g[l].reshape(SUBLANES, LANES), ln2_b[l].reshape(SUBLANES, LANES))

        h = h2.reshape(n_tokens, d_model)

    return h.reshape(seq, batch, d_model).transpose(1, 0, 2)
```

```python
import functools

import jax
import jax.numpy as jnp
from jax import lax
from jax.experimental import pallas as pl
from jax.experimental.pallas import tpu as pltpu

F32 = jnp.float32
BF16 = jnp.bfloat16

SUBLANES = 8
LANES = 128
VMEM_LIMIT_BYTES = 56 * 1024 * 1024

LRU_HEADS = 8
LRU_CONV_W = 4
LRU_C = 8.0
SHORT_CONV_W = 3
PEER_HEADS = 8
PEER_TOPK = 16
LN_EPS = 1e-5
RMS_EPS = 1e-6
SQRT_HALF = 0.7071067811865476

STEPS_PER_CHUNK = 64
ROUTE_TOKENS = 1024
TOKEN_GROUP = 128
PER_PHASE = 8
HALF_ROWS = 4
SLOT_STRIDE = TOKEN_GROUP // 2 + HALF_ROWS


def _layer_norm(x, g, b):
    mu = jnp.mean(x, axis=-1, keepdims=True)
    xc = x - mu
    var = jnp.mean(xc * xc, axis=-1, keepdims=True)
    return xc * lax.rsqrt(var + LN_EPS) * g + b


def _rms_norm(x, g):
    return x * lax.rsqrt(jnp.mean(x * x, axis=-1, keepdims=True) + RMS_EPS) * g


def _gelu(x):
    return 0.5 * x * (1.0 + lax.erf(x * SQRT_HALF))


def _softplus(y):
    return jnp.maximum(y, 0.0) + jnp.log1p(jnp.exp(-jnp.abs(y)))


def _lru_coeffs(xc, wa_ref, ba_ref, wx_ref, bx_ref, lam_ref):
    blk = wa_ref.shape[-1]
    xcb = xc.astype(BF16)
    rs, gs = [], []
    for hd in range(LRU_HEADS):
        xh = xcb[:, hd * blk:(hd + 1) * blk]
        rs.append(jnp.dot(xh, wa_ref[hd], preferred_element_type=F32))
        gs.append(jnp.dot(xh, wx_ref[hd], preferred_element_type=F32))
    r = jax.nn.sigmoid(jnp.concatenate(rs, axis=1) + ba_ref[...])
    gate_i = jax.nn.sigmoid(jnp.concatenate(gs, axis=1) + bx_ref[...])
    log_a = -LRU_C * r * _softplus(-lam_ref[...])
    a = jnp.exp(log_a)
    y = 2.0 * log_a
    e = a * a
    neg_expm1 = jnp.where(e == 1.0, -y, (1.0 - e) * y / jnp.log(e))
    u = jnp.sqrt(neg_expm1) * (gate_i * xc)
    return a, u


def _scan_chunk(a_s, u_s, h_s, steps, reverse):
    unroll = 8

    def body(k, h):
        for q in range(unroll):
            step = k * unroll + q
            if reverse:
                step = steps - 1 - step
            row = pl.multiple_of(step * SUBLANES, SUBLANES)
            h = a_s[pl.ds(row, SUBLANES), :] * h + u_s[pl.ds(row, SUBLANES), :]
            u_s[pl.ds(row, SUBLANES), :] = h
        return h

    h_s[...] = lax.fori_loop(0, steps // unroll, body, h_s[...])


def _lru_bwd_kernel(x_ref, lng_ref, lnb_ref, wxr_ref, cw_ref, cb_ref, wa_ref, ba_ref, wx_ref, bx_ref,
                    lam_ref, hb_ref, a_s, u_s, h_s, halo_s):
    rows = x_ref.shape[0]
    halo_rows = halo_s.shape[0]

    @pl.when(pl.program_id(0) == 0)
    def _():
        h_s[...] = jnp.zeros_like(h_s)
        halo_s[...] = jnp.zeros_like(halo_s)

    h0 = _layer_norm(x_ref[...], lng_ref[...], lnb_ref[...])
    xr = jnp.dot(h0.astype(BF16), wxr_ref[...], preferred_element_type=F32)
    ext = jnp.concatenate([xr, halo_s[...]], axis=0)
    xc = cb_ref[...]
    for k in range(LRU_CONV_W):
        xc = xc + cw_ref[k:k + 1, :] * ext[k * SUBLANES:k * SUBLANES + rows, :]
    halo_s[...] = xr[0:halo_rows, :]
    a, u = _lru_coeffs(xc, wa_ref, ba_ref, wx_ref, bx_ref, lam_ref)
    a_s[...] = a
    u_s[...] = u
    _scan_chunk(a_s, u_s, h_s, rows // SUBLANES, reverse=True)
    hb_ref[...] = u_s[...]


def _mixer_kernel(x_ref, xn_ref, hb_ref, lng_ref, lnb_ref, win_ref, cw_ref, cb_ref, wa_ref, ba_ref,
                  wx_ref, bx_ref, lam_ref, scw_ref, gl_ref, gc_ref, wout_ref, l1g_ref, l1b_ref,
                  o_ref, o3_ref, a_s, u_s, h_s, halo_s, phalo_s, *, alpha, d_lru, d_conv):
    rows = x_ref.shape[0]
    halo_rows = halo_s.shape[0]
    i = pl.program_id(0)

    @pl.when(i == 0)
    def _():
        h_s[...] = jnp.zeros_like(h_s)
        halo_s[...] = jnp.zeros_like(halo_s)
        phalo_s[...] = jnp.zeros_like(phalo_s)

    h0 = _layer_norm(x_ref[...], lng_ref[...], lnb_ref[...])
    z = jnp.dot(h0.astype(BF16), win_ref[...], preferred_element_type=F32)
    xr = z[:, 0:d_lru]
    gate = z[:, d_lru:2 * d_lru]
    c0 = 2 * d_lru
    cb = z[:, c0:c0 + d_conv]
    cc = z[:, c0 + d_conv:c0 + 2 * d_conv]
    cx = z[:, c0 + 2 * d_conv:c0 + 3 * d_conv]

    ext = jnp.concatenate([halo_s[...], xr], axis=0)
    xc = cb_ref[...]
    for k in range(LRU_CONV_W):
        xc = xc + cw_ref[k:k + 1, :] * ext[k * SUBLANES:k * SUBLANES + rows, :]
    halo_s[...] = xr[rows - halo_rows:rows, :]
    a, u = _lru_coeffs(xc, wa_ref, ba_ref, wx_ref, bx_ref, lam_ref)
    a_s[...] = a
    u_s[...] = u
    _scan_chunk(a_s, u_s, h_s, rows // SUBLANES, reverse=False)
    y_lru = _gelu(gate) * (u_s[...] + hb_ref[...])

    p = cc * cx
    h0n = _layer_norm(xn_ref[...], lng_ref[...], lnb_ref[...])
    zn = jnp.dot(h0n.astype(BF16), win_ref[:, c0 + d_conv:c0 + 3 * d_conv], preferred_element_type=F32)
    pn = zn[:, 0:d_conv] * zn[:, d_conv:2 * d_conv]
    pn = jnp.where(i == pl.num_programs(0) - 1, 0.0, pn)
    pext = jnp.concatenate([phalo_s[...], p, pn], axis=0)
    yc = scw_ref[0:1, :] * pext[0:rows, :]
    for k in range(1, SHORT_CONV_W):
        yc = yc + scw_ref[k:k + 1, :] * pext[k * SUBLANES:k * SUBLANES + rows, :]
    phalo_s[...] = p[rows - SUBLANES:rows, :]
    y_conv = cb * yc

    ml = _rms_norm(y_lru, gl_ref[...]).astype(BF16)
    mc = _rms_norm(y_conv, gc_ref[...]).astype(BF16)
    o = (jnp.dot(ml, wout_ref[0:d_lru, :], preferred_element_type=F32)
         + jnp.dot(mc, wout_ref[d_lru:d_lru + d_conv, :], preferred_element_type=F32))
    h1 = _layer_norm(alpha * h0 + o, l1g_ref[...], l1b_ref[...])
    o_ref[...] = h1.astype(BF16)
    o3_ref[...] = h1.reshape(o3_ref.shape)


def _sorting_network(n):
    pairs = []
    p = 1
    while p < n:
        k = p
        while k >= 1:
            for j in range(k % p, n - k, 2 * k):
                for i in range(min(k, n - j - k)):
                    if (i + j) // (2 * p) == (i + j + k) // (2 * p):
                        pairs.append((i + j, i + j + k))
            k //= 2
        p *= 2
    return pairs


def _top_rows(s, count, order=None, payload=None):
    n, lanes = s.shape
    levels = n // SUBLANES
    tiles = lambda x: [x[k * SUBLANES:(k + 1) * SUBLANES, :] for k in range(levels)]
    v = tiles(s)
    if order is None:
        sub = lax.broadcasted_iota(jnp.int32, (SUBLANES, lanes), 0)
        od = [sub + k * SUBLANES for k in range(levels)]
    else:
        od = tiles(order)
    cols = [v, od] + ([tiles(payload)] if payload is not None else [])
    for a, b in _sorting_network(levels):
        swap = (v[b] > v[a]) | ((v[b] == v[a]) & (od[b] < od[a]))
        for col in cols:
            col[a], col[b] = jnp.where(swap, col[b], col[a]), jnp.where(swap, col[a], col[b])
    vals, outs = [], []
    for r in range(count):
        m = jnp.max(v[0], axis=0, keepdims=True)
        sel = jnp.min(jnp.where(v[0] == m, od[0], jnp.iinfo(jnp.int32).max), axis=0, keepdims=True)
        pop = od[0] == sel
        vals.append(m)
        outs.append(sel if payload is None else jnp.max(jnp.where(pop, cols[2][0], -1), axis=0, keepdims=True))
        depth = min(levels - 1, count - 1 - r)
        for col in cols:
            for k in range(depth):
                col[k] = jnp.where(pop, col[k + 1], col[k])
        if depth == levels - 1:
            v[depth] = jnp.where(pop, -jnp.inf, v[depth])
    return jnp.concatenate(vals, axis=0), jnp.concatenate(outs, axis=0)


def _route_kernel(h_ref, wq_ref, k1_ref, k2_ref, idx_ref, g_ref, *, n_keys):
    groups = idx_ref.shape[0]
    half = k1_ref.shape[1]
    q = jnp.dot(h_ref[...], wq_ref[...], preferred_element_type=F32).astype(BF16)
    nt = (((1,), (1,)), ((), ()))
    s1 = lax.dot_general(k1_ref[...], q[:, 0:half], nt, preferred_element_type=F32)
    s2 = lax.dot_general(k2_ref[...], q[:, half:2 * half], nt, preferred_element_type=F32)

    for c in range(groups):
        v1, i1 = _top_rows(s1[:, c * LANES:(c + 1) * LANES], PEER_TOPK)
        v2, i2 = _top_rows(s2[:, c * LANES:(c + 1) * LANES], PEER_TOPK)
        iota8 = lax.broadcasted_iota(jnp.int32, (SUBLANES, LANES), 0)
        cands, cand_es, poss = [], [], []
        for b in range(SUBLANES):
            for ab in range(PEER_TOPK // SUBLANES):
                if (ab * SUBLANES + 1) * (b + 1) <= PEER_TOPK:
                    rows = slice(ab * SUBLANES, (ab + 1) * SUBLANES)
                    cands.append(v1[rows, :] + v2[b:b + 1, :])
                    cand_es.append(i1[rows, :] * n_keys + i2[b:b + 1, :])
                    poss.append((iota8 + ab * SUBLANES) * PEER_TOPK + b)
        for a in range(PEER_TOPK):
            for bb in range(1, PEER_TOPK // SUBLANES):
                if (a + 1) * (bb * SUBLANES + 1) <= PEER_TOPK:
                    rows = slice(bb * SUBLANES, (bb + 1) * SUBLANES)
                    cands.append(v1[a:a + 1, :] + v2[rows, :])
                    cand_es.append(i1[a:a + 1, :] * n_keys + i2[rows, :])
                    poss.append(a * PEER_TOPK + bb * SUBLANES + iota8)
        cand = jnp.concatenate(cands, axis=0)
        cand_e = jnp.concatenate(cand_es, axis=0)
        cpos = jnp.concatenate(poss, axis=0)
        top, ids = _top_rows(cand, PEER_TOPK, order=cpos, payload=cand_e)
        ex = jnp.exp(top - top[0:1, :])
        g_ref[c] = ex / jnp.sum(ex, axis=0, keepdims=True)
        rank = lax.broadcasted_iota(jnp.int32, (PEER_TOPK, LANES), 0)
        lead = jnp.where((rank & 1) == 0, HALF_ROWS, 0)
        idx_ref[c] = ids * HALF_ROWS + lead


def _unpack(words):
    lo = lax.bitcast_convert_type(words << 16, F32)
    hi = lax.bitcast_convert_type(words & jnp.uint32(0xFFFF0000), F32)
    return lo, hi


def _couple_rows(idx_ref, offs, t):
    n = len(offs)
    for m in range(idx_ref.shape[2] // n):
        view = idx_ref.at[0, t, pl.ds(n * m, n)]
        for q in range(n // 2):
            yield ((n // 2) * m + q, pl.multiple_of(view[offs[2 * q]], HALF_ROWS),
                   pl.multiple_of(view[offs[2 * q + 1]], HALF_ROWS))


def _gather_couple(tab_ref, upper, row_a, row_b):
    return jnp.where(upper, tab_ref[pl.ds(row_b, SUBLANES), :], tab_ref[pl.ds(row_a, SUBLANES), :])


def _fill_slabs(idx_ref, offs, tab_ref, slot, upper, t):
    for c, row_a, row_b in _couple_rows(idx_ref, offs, t):
        slot[pl.ds(c, SUBLANES, stride=SLOT_STRIDE), :] = _gather_couple(tab_ref, upper, row_a, row_b)


def _slab(slot, s, couples):
    even_pairs = pltpu.bitcast(slot[s * SLOT_STRIDE:s * SLOT_STRIDE + couples, :], BF16)
    odd_pairs = pltpu.bitcast(slot[(HALF_ROWS + s) * SLOT_STRIDE:(HALF_ROWS + s) * SLOT_STRIDE + couples, :], BF16)
    return jnp.concatenate([even_pairs, odd_pairs], axis=0)


def _split3(v):
    p0 = v.astype(BF16).astype(F32)
    r1 = v - p0
    p1 = r1.astype(BF16).astype(F32)
    return p0, p1, (r1 - p1).astype(BF16).astype(F32)


def _u_couples(idx_ref, offs, x_ref, tab_ref, slot, upper, t):
    xt = x_ref[t]
    x_lo = jnp.concatenate([xt[0:HALF_ROWS, :]] * 2, axis=0)
    x_hi = jnp.concatenate([xt[HALF_ROWS:SUBLANES, :]] * 2, axis=0)
    for c, row_a, row_b in _couple_rows(idx_ref, offs, t):
        lo, hi = _unpack(_gather_couple(tab_ref, upper, row_a, row_b))
        slot[pl.ds(c, SUBLANES, stride=SLOT_STRIDE), :] = lo * x_lo + hi * x_hi


def _u_dots(slot, half):
    slabs = [slot[s * SLOT_STRIDE:s * SLOT_STRIDE + half, :] for s in range(SUBLANES)]
    part_a = (slabs[0] + slabs[1]) + (slabs[2] + slabs[3])
    part_b = (slabs[4] + slabs[5]) + (slabs[6] + slabs[7])
    return jnp.sum(jnp.concatenate([part_a, part_b], axis=0), axis=1, keepdims=True)


def _expert_u_kernel(offs_ref, idx_ref, x_ref, g_ref, tab_ref, w_ref, slot_a, slot_b):
    n_pairs = idx_ref.shape[2]
    offs = [offs_ref[q] for q in range(offs_ref.shape[0])]
    lane = lax.broadcasted_iota(jnp.int32, (n_pairs, TOKEN_GROUP), 1)
    upper = lax.broadcasted_iota(jnp.int32, (SUBLANES, LANES), 0) >= HALF_ROWS
    @pl.when(pl.program_id(0) == 0)
    def _():
        slot_b[...] = jnp.zeros_like(slot_b)

    def fill(slot, t0):
        for k in range(PER_PHASE):
            _u_couples(idx_ref, offs, x_ref, tab_ref, slot.at[k], upper, t0 + k)

    def reduce(slot, t0, act):
        for k in range(PER_PHASE):
            act = jnp.where(lane == t0 + k, _u_dots(slot.at[k], n_pairs // 2), act)
        return act

    def step(i, act):
        t0 = 2 * PER_PHASE * i
        fill(slot_a, t0)
        act = reduce(slot_b, t0 - PER_PHASE, act)
        fill(slot_b, t0 + PER_PHASE)
        return reduce(slot_a, t0, act)

    act = lax.fori_loop(0, TOKEN_GROUP // (2 * PER_PHASE), step, jnp.zeros((n_pairs, TOKEN_GROUP), F32))
    act = reduce(slot_b, TOKEN_GROUP - PER_PHASE, act)
    w_ref[0] = g_ref[0] * _gelu(act)


def _v_token(slot, w_row, couples):
    n = 2 * couples
    w3 = jnp.concatenate(list(_split3(w_row)) + [jnp.zeros_like(w_row)], axis=0)
    even_lane = (lax.broadcasted_iota(jnp.int32, w3.shape, 1) & 1) == 0
    lo = jnp.concatenate([jnp.where(even_lane, w3, 0.0),
                          jnp.where(even_lane, pltpu.roll(w3, n - 1, axis=1), 0.0)], axis=1)
    hi = jnp.concatenate([jnp.where(even_lane, 0.0, pltpu.roll(w3, 1, axis=1)),
                          jnp.where(even_lane, 0.0, w3)], axis=1)
    lhs = jnp.concatenate([lo, hi], axis=0).astype(BF16)
    rows_lo, rows_hi = [], []
    for s in range(HALF_ROWS):
        o = jnp.dot(lhs, _slab(slot, s, couples), preferred_element_type=F32)
        rows_lo.append((o[0:1, :] + o[1:2, :]) + o[2:3, :])
        rows_hi.append((o[4:5, :] + o[5:6, :]) + o[6:7, :])
    return jnp.concatenate(rows_lo + rows_hi, axis=0)


def _expert_v_kernel(offs_ref, idx_ref, w_ref, x_ref, tab_ref, lng_ref, lnb_ref, o_ref, slot_a, slot_b, y_s, *, alpha):
    couples = idx_ref.shape[2] // 2
    offs = [offs_ref[q] for q in range(offs_ref.shape[0])]
    upper = lax.broadcasted_iota(jnp.int32, (SUBLANES, LANES), 0) >= HALF_ROWS
    @pl.when(pl.program_id(0) == 0)
    def _():
        slot_b[...] = jnp.zeros_like(slot_b)

    def fill(slot, t0):
        for k in range(PER_PHASE):
            _fill_slabs(idx_ref, offs, tab_ref, slot.at[k], upper, t0 + k)

    def combine(slot, t0):
        for k in range(PER_PHASE):
            t = jnp.maximum(t0 + k, 0)
            y = _v_token(slot.at[k], w_ref[0, pl.ds(t, 1), :], couples)
            y_s[t & (SUBLANES - 1), lax.shift_right_logical(t, 3)] = alpha * x_ref[t] + y

    def step(i, carry):
        t0 = 2 * PER_PHASE * i
        fill(slot_a, t0)
        combine(slot_b, t0 - PER_PHASE)
        fill(slot_b, t0 + PER_PHASE)
        combine(slot_a, t0)
        return carry

    lax.fori_loop(0, TOKEN_GROUP // (2 * PER_PHASE), step, 0)
    combine(slot_b, TOKEN_GROUP - PER_PHASE)
    h = y_s[...]
    d = h.shape[2] * h.shape[3]
    mu = jnp.sum(jnp.sum(h, axis=3, keepdims=True), axis=2, keepdims=True) / d
    hc = h - mu
    var = jnp.sum(jnp.sum(hc * hc, axis=3, keepdims=True), axis=2, keepdims=True) / d
    out = hc * lax.rsqrt(var + LN_EPS) * lng_ref[...] + lnb_ref[...]
    o_ref[...] = out.reshape(o_ref.shape)


def _pack_table(tab):
    e, d = tab.shape
    bits = lambda x: lax.bitcast_convert_type(x.astype(BF16), jnp.uint16).astype(jnp.uint32)
    words = bits(tab[:, 0:d // 2]) | (bits(tab[:, d // 2:d]) << 16)
    words = words.reshape(e * (d // 2 // LANES), LANES)
    return jnp.pad(words, ((HALF_ROWS, HALF_ROWS), (0, 0)))


def _const_spec(shape):
    zeros = (0,) * len(shape)
    return pl.BlockSpec(shape, lambda *_: zeros, pipeline_mode=pl.Buffered(1))


def _row(v):
    return v.reshape(1, -1).astype(F32)


def _params(semantics):
    return pltpu.CompilerParams(dimension_semantics=semantics, vmem_limit_bytes=VMEM_LIMIT_BYTES)


def kernel(x, ln_emb_g, ln_emb_b, w_in, lru_conv_w, lru_conv_b, lru_wa, lru_ba, lru_wx, lru_bx, lru_lambda, conv_w, norm_lru_g, norm_conv_g, w_out, ln1_g, ln1_b, peer_wq, peer_k1, peer_k2, peer_u, peer_v, ln2_g, ln2_b):
    batch, seq, d_model = x.shape
    depth = w_in.shape[0]
    assert batch == SUBLANES and d_model == SUBLANES * LANES
    assert depth == 1
    d_lru = lru_conv_w.shape[-1]
    d_conv = conv_w.shape[-1]
    n_keys, half = peer_k1.shape[1], peer_k1.shape[2]
    n_tokens = batch * seq
    steps = min(STEPS_PER_CHUNK, seq)
    rows = steps * batch
    n_chunks = seq // steps
    assert seq % steps == 0 and steps % 8 == 0
    route_tokens = min(ROUTE_TOKENS, n_tokens)
    assert n_tokens % route_tokens == 0 and route_tokens % TOKEN_GROUP == 0
    n_groups = n_tokens // TOKEN_GROUP
    halo_rows = (LRU_CONV_W - 1) * batch
    alpha = (2.0 * depth) ** 0.25

    h = x.transpose(1, 0, 2).reshape(n_tokens, d_model)
    ln_g, ln_b = _row(ln_emb_g), _row(ln_emb_b)

    for l in range(depth):
        w_in_b = w_in[l].astype(BF16)
        wa_b, wx_b = lru_wa[l].astype(BF16), lru_wx[l].astype(BF16)
        lru_small = lambda d: [lru_conv_w[l, d], _row(lru_conv_b[l, d]), wa_b[d], _row(lru_ba[l, d]),
                               wx_b[d], _row(lru_bx[l, d]), _row(lru_lambda[l, d])]
        lru_specs = [_const_spec((LRU_CONV_W, d_lru)), _const_spec((1, d_lru)),
                     _const_spec(wa_b.shape[1:]), _const_spec((1, d_lru)),
                     _const_spec(wx_b.shape[1:]), _const_spec((1, d_lru)), _const_spec((1, d_lru))]
        lru_scratch = [pltpu.VMEM((rows, d_lru), F32), pltpu.VMEM((rows, d_lru), F32),
                       pltpu.VMEM((batch, d_lru), F32), pltpu.VMEM((halo_rows, d_lru), F32)]

        hb = pl.pallas_call(
            _lru_bwd_kernel,
            grid=(n_chunks,),
            in_specs=[pl.BlockSpec((rows, d_model), lambda i: (n_chunks - 1 - i, 0)),
                      _const_spec((1, d_model)), _const_spec((1, d_model)),
                      _const_spec((d_model, d_lru))] + lru_specs,
            out_specs=pl.BlockSpec((rows, d_lru), lambda i: (n_chunks - 1 - i, 0)),
            out_shape=jax.ShapeDtypeStruct((n_tokens, d_lru), F32),
            scratch_shapes=lru_scratch,
            compiler_params=_params(("arbitrary",)),
            name="lru_bwd",
        )(h, ln_g, ln_b, w_in_b[:, 0:d_lru], *lru_small(1))

        h1, h1_3 = pl.pallas_call(
            functools.partial(_mixer_kernel, alpha=alpha, d_lru=d_lru, d_conv=d_conv),
            grid=(n_chunks,),
            in_specs=[pl.BlockSpec((rows, d_model), lambda i: (i, 0)),
                      pl.BlockSpec((batch, d_model), lambda i: (jnp.minimum((i + 1) * steps, seq - 1), 0)),
                      pl.BlockSpec((rows, d_lru), lambda i: (i, 0)),
                      _const_spec((1, d_model)), _const_spec((1, d_model)),
                      _const_spec(w_in_b.shape)] + lru_specs +
                     [_const_spec((SHORT_CONV_W, d_conv)), _const_spec((1, d_lru)), _const_spec((1, d_conv)),
                      _const_spec(w_out.shape[1:]), _const_spec((1, d_model)), _const_spec((1, d_model))],
            out_specs=[pl.BlockSpec((rows, d_model), lambda i: (i, 0)),
                       pl.BlockSpec((rows, SUBLANES, LANES), lambda i: (i, 0, 0))],
            out_shape=[jax.ShapeDtypeStruct((n_tokens, d_model), BF16),
                       jax.ShapeDtypeStruct((n_tokens, SUBLANES, LANES), F32)],
            scratch_shapes=lru_scratch + [pltpu.VMEM((batch, d_conv), F32)],
            compiler_params=_params(("arbitrary",)),
            name="mixer",
        )(h, h, hb, ln_g, ln_b, w_in_b, *lru_small(0), conv_w[l], _row(norm_lru_g[l]), _row(norm_conv_g[l]),
          w_out[l].astype(BF16), _row(ln1_g[l]), _row(ln1_b[l]))

        groups_per_step = route_tokens // TOKEN_GROUP
        n_pairs = PEER_HEADS * PEER_TOPK
        idx3, g3 = pl.pallas_call(
            functools.partial(_route_kernel, n_keys=n_keys),
            grid=(n_tokens // route_tokens, PEER_HEADS),
            in_specs=[pl.BlockSpec((route_tokens, d_model), lambda i, hd: (i, 0)),
                      pl.BlockSpec((d_model, 2 * half), lambda i, hd: (0, hd)),
                      _const_spec((n_keys, half)), _const_spec((n_keys, half))],
            out_specs=[pl.BlockSpec((groups_per_step, PEER_TOPK, TOKEN_GROUP), lambda i, hd: (i, hd, 0)),
                       pl.BlockSpec((groups_per_step, PEER_TOPK, TOKEN_GROUP), lambda i, hd: (i, hd, 0))],
            out_shape=[jax.ShapeDtypeStruct((n_groups, n_pairs, TOKEN_GROUP), jnp.int32),
                       jax.ShapeDtypeStruct((n_groups, n_pairs, TOKEN_GROUP), F32)],
            compiler_params=_params(("arbitrary", "arbitrary")),
            name="route",
        )(h1, peer_wq[l].astype(BF16), peer_k1[l].astype(BF16), peer_k2[l].astype(BF16))

        u_tab = _pack_table(peer_u[l])
        v_tab = _pack_table(peer_v[l])
        idx3_t = idx3.swapaxes(1, 2)
        g3_eo = g3.reshape(n_groups, n_pairs // 2, 2, TOKEN_GROUP).swapaxes(1, 2).reshape(g3.shape)
        assert n_pairs == TOKEN_GROUP
        group_spec = lambda space: pl.BlockSpec((1, n_pairs, TOKEN_GROUP), lambda i: (i, 0, 0), memory_space=space)
        tok_spec = pl.BlockSpec((TOKEN_GROUP, SUBLANES, LANES), lambda i: (i, 0, 0))
        offs = jnp.arange(SUBLANES, dtype=jnp.int32)
        offs_spec = pl.BlockSpec(memory_space=pltpu.SMEM)
        slot_scratch = pltpu.VMEM((PER_PHASE, SUBLANES * SLOT_STRIDE, LANES), jnp.uint32)

        w3 = pl.pallas_call(
            _expert_u_kernel,
            grid=(n_groups,),
            in_specs=[offs_spec, group_spec(pltpu.SMEM), tok_spec, group_spec(pltpu.VMEM), _const_spec(u_tab.shape)],
            out_specs=group_spec(pltpu.VMEM),
            out_shape=jax.ShapeDtypeStruct((n_groups, n_pairs, TOKEN_GROUP), F32),
            scratch_shapes=[pltpu.VMEM((PER_PHASE, SUBLANES * SLOT_STRIDE, LANES), F32),
                            pltpu.VMEM((PER_PHASE, SUBLANES * SLOT_STRIDE, LANES), F32)],
            compiler_params=_params(("arbitrary",)),
            name="expert_u",
        )(offs, idx3_t, h1_3, g3_eo, u_tab)
        w3_t = w3.reshape(n_groups, 2, n_pairs // 2, TOKEN_GROUP).transpose(0, 3, 2, 1).reshape(
            n_groups, TOKEN_GROUP, n_pairs)

        h2 = pl.pallas_call(
            functools.partial(_expert_v_kernel, alpha=alpha),
            grid=(n_groups,),
            in_specs=[offs_spec, group_spec(pltpu.SMEM), group_spec(pltpu.VMEM), tok_spec, _const_spec(v_tab.shape),
                      _const_spec((SUBLANES, LANES)), _const_spec((SUBLANES, LANES))],
            out_specs=pl.BlockSpec((batch, TOKEN_GROUP // batch, d_model), lambda i: (0, i, 0)),
            out_shape=jax.ShapeDtypeStruct((batch, seq, d_model), F32),
            scratch_shapes=[slot_scratch, slot_scratch,
                            pltpu.VMEM((batch, TOKEN_GROUP // batch, SUBLANES, LANES), F32)],
            compiler_params=_params(("arbitrary",)),
            name="expert_v",
        )(offs, idx3_t, w3_t, h1_3, v_tab, ln2_g[l].reshape(SUBLANES, LANES), ln2_b[l].reshape(SUBLANES, LANES))

    return h2
```
